```python
import jax, jax.numpy as jnp
from jax import lax
import numpy as np

D_MODEL = 1024
BATCH = 4
SEQ = 4096
DEPTH = 1

MEM_LEN = 256
HG_HEADS = 4
HG_DK = 128
HG_DV = 128
HG_WIDTH = HG_HEADS * HG_DV
HG_CHUNK = 64
SG_GROUPS = 4
SG_DIM = 128
SG_WIDTH = SG_GROUPS * SG_DIM
SG_CHUNK = 128
MIX_WIDTH = HG_WIDTH + SG_WIDTH
IN_WIDTH = 4 * HG_WIDTH + 2 * SG_WIDTH
X_HEADS = 4
X_HEAD_DIM = D_MODEL // X_HEADS
D_FF = 2816
ALPHA = (2.0 * DEPTH) ** 0.25
BETA = (8.0 * DEPTH) ** -0.25
LN_EPS = 1e-5

kernel_name = "hybrid_hgrn2_sgu_macaron_deepnorm"


def _layer_norm(x, g, b):
    xf = x.astype(jnp.float32)
    mu = jnp.mean(xf, axis=-1, keepdims=True)
    var = jnp.mean(jnp.square(xf - mu), axis=-1, keepdims=True)
    return ((xf - mu) * lax.rsqrt(var + LN_EPS) * g + b).astype(x.dtype)


def _rms_norm(x, g):
    xf = x.astype(jnp.float32)
    return xf * lax.rsqrt(jnp.mean(jnp.square(xf), axis=-1, keepdims=True) + LN_EPS) * g


def _swiglu(x, w_gate, w_up, w_down):
    return (jax.nn.silu(x @ w_gate) * (x @ w_up)) @ w_down


def _hgrn2(q, fz, iv, lb):
    B, T, H, DK = q.shape
    DV = iv.shape[-1]
    n_chunks = T // HG_CHUNK
    q = q.astype(jnp.float32)
    fz = fz.astype(jnp.float32)
    iv = iv.astype(jnp.float32)
    log_f = jnp.log(lb + (1.0 - lb) * jax.nn.sigmoid(fz))
    k = (1.0 - lb) * jax.nn.sigmoid(-fz)

    def chunks(a):
        return a.reshape(B, n_chunks, HG_CHUNK, H, a.shape[-1]).transpose(1, 0, 3, 2, 4)

    causal = jnp.tril(jnp.ones((HG_CHUNK, HG_CHUNK), dtype=bool))[:, :, None]

    def step(S, inp):
        qc, kc, vc, lfc = inp
        b = jnp.cumsum(lfc, axis=2)
        diff = b[:, :, :, None, :] - b[:, :, None, :, :]
        decay = jnp.where(causal, jnp.exp(jnp.where(causal, diff, 0.0)), 0.0)
        scores = jnp.einsum('bhtd,bhsd,bhtsd->bhts', qc, kc, decay)
        o = (jnp.einsum('bhts,bhsv->bhtv', scores, vc)
             + jnp.einsum('bhtd,bhdv->bhtv', qc * jnp.exp(b), S))
        b_end = b[:, :, -1:, :]
        S = (jnp.exp(b_end[:, :, 0, :])[..., None] * S
             + jnp.einsum('bhsd,bhsv->bhdv', kc * jnp.exp(b_end - b), vc))
        return S, o

    S0 = jnp.zeros((B, H, DK, DV), jnp.float32)
    _, o = lax.scan(step, S0, (chunks(q), chunks(k), chunks(iv), chunks(log_f)))
    return o.transpose(1, 0, 3, 2, 4).reshape(B, T, H, DV)


def _spatial_gating(uv, w_s, b_s, ln_g, ln_b):
    B, T, _ = uv.shape
    u, v = jnp.split(uv, 2, axis=-1)
    v = _layer_norm(v.reshape(B, T, SG_GROUPS, SG_DIM), ln_g, ln_b)
    v = v.reshape(B, T // SG_CHUNK, SG_CHUNK, SG_GROUPS, SG_DIM)
    causal = jnp.tril(jnp.ones((SG_CHUNK, SG_CHUNK), dtype=bool))
    w = jnp.where(causal, w_s, 0.0)
    s = jnp.einsum('gts,bnsgc->bntgc', w, v) + b_s.T[None, None, :, :, None]
    return u * s.reshape(B, T, SG_WIDTH)


def _token_mixers(h, w_in, lb, hg_norm_g, sg_ln_g, sg_ln_b, sg_w_s, sg_b_s, w_out):
    B, T, _ = h.shape
    proj = h @ w_in
    q, fz, iv, g, uv = jnp.split(
        proj, [HG_WIDTH, 2 * HG_WIDTH, 3 * HG_WIDTH, 4 * HG_WIDTH], axis=-1)
    heads = lambda a: a.reshape(B, T, HG_HEADS, -1)
    o = _hgrn2(heads(q), heads(fz), heads(iv), lb)
    o = _rms_norm(o, hg_norm_g) * jax.nn.silu(heads(g).astype(jnp.float32))
    o_a = o.reshape(B, T, HG_WIDTH).astype(h.dtype)
    o_b = _spatial_gating(jax.nn.gelu(uv), sg_w_s, sg_b_s, sg_ln_g, sg_ln_b)
    return jnp.concatenate([o_a, o_b], axis=-1) @ w_out


def _memory_cross_attention(h, mem, mem_g, mem_b, wq, wk, wv, wo):
    B, T, _ = h.shape
    M = mem.shape[1]
    m = _layer_norm(mem, mem_g, mem_b)
    q = (h @ wq).reshape(B, T, X_HEADS, X_HEAD_DIM)
    k = (m @ wk).reshape(B, M, X_HEADS, X_HEAD_DIM)
    v = (m @ wv).reshape(B, M, X_HEADS, X_HEAD_DIM)
    s = jnp.einsum('bthd,bmhd->bhtm', q.astype(jnp.float32), k.astype(jnp.float32)) * (X_HEAD_DIM ** -0.5)
    p = jax.nn.softmax(s, axis=-1).astype(h.dtype)
    o = jnp.einsum('bhtm,bmhd->bthd', p, v).reshape(B, T, D_MODEL)
    return o @ wo


def setup_inputs(seed: int = 0) -> dict:
    key = jax.random.key(seed)
    ks = iter(jax.random.split(key, 48))
    L = DEPTH

    def nrm(shape, scale):
        return jax.random.normal(next(ks), shape, jnp.float32) * scale

    def gain(shape):
        return 1.0 + nrm(shape, 0.05)

    def bias(shape):
        return nrm(shape, 0.01)

    d_in = D_MODEL ** -0.5
    f_in = D_FF ** -0.5
    return {
        "x": nrm((BATCH, SEQ, D_MODEL), 1.0),
        "mem": nrm((BATCH, MEM_LEN, D_MODEL), 1.0),
        "ffn1_w_gate": nrm((L, D_MODEL, D_FF), d_in),
        "ffn1_w_up": nrm((L, D_MODEL, D_FF), d_in),
        "ffn1_w_down": nrm((L, D_FF, D_MODEL), f_in * BETA),
        "ln1_g": gain((L, D_MODEL)),
        "ln1_b": bias((L, D_MODEL)),
        "w_in": nrm((L, D_MODEL, IN_WIDTH), d_in),
        "hg_lb_logits": nrm((DEPTH + 1, HG_HEADS, HG_DK), 0.5),
        "hg_norm_g": gain((L, HG_DV)),
        "sg_ln_g": gain((L, SG_GROUPS, SG_DIM)),
        "sg_ln_b": bias((L, SG_GROUPS, SG_DIM)),
        "sg_w_s": nrm((L, SG_GROUPS, SG_CHUNK, SG_CHUNK), SG_CHUNK ** -0.5),
        "sg_b_s": gain((L, SG_GROUPS, SG_CHUNK)),
        "w_out": nrm((L, MIX_WIDTH, D_MODEL), (MIX_WIDTH ** -0.5) * BETA),
        "ln2_g": gain((L, D_MODEL)),
        "ln2_b": bias((L, D_MODEL)),
        "mem_ln_g": gain((L, D_MODEL)),
        "mem_ln_b": bias((L, D_MODEL)),
        "xa_w_q": nrm((L, D_MODEL, D_MODEL), d_in),
        "xa_w_k": nrm((L, D_MODEL, D_MODEL), d_in),
        "xa_w_v": nrm((L, D_MODEL, D_MODEL), d_in * BETA),
        "xa_w_o": nrm((L, D_MODEL, D_MODEL), d_in * BETA),
        "ln3_g": gain((L, D_MODEL)),
        "ln3_b": bias((L, D_MODEL)),
        "ffn2_w_gate": nrm((L, D_MODEL, D_FF), d_in),
        "ffn2_w_up": nrm((L, D_MODEL, D_FF), d_in),
        "ffn2_w_down": nrm((L, D_FF, D_MODEL), f_in * BETA),
        "ln4_g": gain((L, D_MODEL)),
        "ln4_b": bias((L, D_MODEL)),
    }


def reference(x, mem, ffn1_w_gate, ffn1_w_up, ffn1_w_down, ln1_g, ln1_b,
              w_in, hg_lb_logits, hg_norm_g, sg_ln_g, sg_ln_b, sg_w_s, sg_b_s,
              w_out, ln2_g, ln2_b, mem_ln_g, mem_ln_b, xa_w_q, xa_w_k, xa_w_v,
              xa_w_o, ln3_g, ln3_b, ffn2_w_gate, ffn2_w_up, ffn2_w_down,
              ln4_g, ln4_b):
    lower_bounds = jnp.cumsum(jax.nn.softmax(hg_lb_logits.astype(jnp.float32), axis=0), axis=0)
    h = x
    for l in range(DEPTH):
        h = _layer_norm(ALPHA * h + 0.5 * _swiglu(h, ffn1_w_gate[l], ffn1_w_up[l], ffn1_w_down[l]),
                        ln1_g[l], ln1_b[l])
        mix = _token_mixers(h, w_in[l], lower_bounds[l], hg_norm_g[l], sg_ln_g[l], sg_ln_b[l],
                            sg_w_s[l], sg_b_s[l], w_out[l])
        h = _layer_norm(ALPHA * h + mix, ln2_g[l], ln2_b[l])
        xa = _memory_cross_attention(h, mem, mem_ln_g[l], mem_ln_b[l], xa_w_q[l], xa_w_k[l],
                                     xa_w_v[l], xa_w_o[l])
        h = _layer_norm(ALPHA * h + xa, ln3_g[l], ln3_b[l])
        h = _layer_norm(ALPHA * h + 0.5 * _swiglu(h, ffn2_w_gate[l], ffn2_w_up[l], ffn2_w_down[l]),
                        ln4_g[l], ln4_b[l])
    return h
```

```python
import functools

import jax
import jax.numpy as jnp
from jax import lax
from jax.experimental import pallas as pl
from jax.experimental.pallas import tpu as pltpu

F32 = jnp.float32
BF16 = jnp.bfloat16

LN_EPS = 1e-5
MEM_LEN = 256
HG_HEADS = 4
HG_DK = 128
HG_WIDTH = HG_HEADS * HG_DK
SG_GROUPS = 4
SG_DIM = 128
SG_WIDTH = SG_GROUPS * SG_DIM
SG_CHUNK = 128
X_HEADS = 4

HG_CHUNK = 64
HG_SUB = 8
HG_LEVELS = (8, 16, 32)

FFN_ROWS = 512
FFN_COLS = 256
MIX_ROWS = 256
VMEM_LIMIT_BYTES = 56 * 1024 * 1024


def _dot(a, b):
    return jnp.dot(a, b, preferred_element_type=F32)


def _dot_nt(a, b):
    return lax.dot_general(a, b, (((1,), (1,)), ((), ())), preferred_element_type=F32)


def _dot_tn(a, b):
    return lax.dot_general(a, b, (((0,), (0,)), ((), ())), preferred_element_type=F32)


def _layer_norm(y, g, b):
    mu = jnp.mean(y, axis=-1, keepdims=True)
    d = y - mu
    var = jnp.mean(d * d, axis=-1, keepdims=True)
    return d * lax.rsqrt(var + LN_EPS) * g + b


def _sigmoid(x):
    return 1.0 / (1.0 + jnp.exp(-x))


def _swiglu_ln(x, alpha, wg_ref, wu_ref, wd_ref, g_ref, b_ref, h_scr):
    xb = x.astype(BF16)
    d_ff = wg_ref.shape[1]
    for j in range(d_ff // FFN_COLS):
        sl = slice(j * FFN_COLS, (j + 1) * FFN_COLS)
        gate = _dot(xb, wg_ref[:, sl])
        up = _dot(xb, wu_ref[:, sl])
        h_scr[:, sl] = (gate * _sigmoid(gate) * up).astype(BF16)
    y = _dot(h_scr[...], wd_ref[...])
    return _layer_norm(alpha * x + 0.5 * y, g_ref[...], b_ref[...])


def _kv_kernel(mem_ref, g_ref, b_ref, wk_ref, wv_ref, k_ref, v_ref):
    m = _layer_norm(mem_ref[...], g_ref[...], b_ref[...]).astype(BF16)
    k_ref[...] = _dot(m, wk_ref[...]).astype(BF16)
    v_ref[...] = _dot(m, wv_ref[...]).astype(BF16)


def _ffn_kernel(alpha, x_ref, wg_ref, wu_ref, wd_ref, g_ref, b_ref, o_ref, h_scr):
    o_ref[...] = _swiglu_ln(x_ref[...], alpha, wg_ref, wu_ref, wd_ref, g_ref, b_ref, h_scr)


def _bcast_row(a, blk, r):
    rows, w = a.shape
    a3 = a.reshape(rows // blk, blk, w)
    return jnp.broadcast_to(a3[:, r:r + 1, :], a3.shape).reshape(rows, w)


def _mixer_kernel(alpha, h_ref, w_in_ref, lbl_ref, hgn_ref, sglg_ref, sglb_ref, ws_ref,
                  bst_ref, w_out_ref, g_ref, b_ref, o_ref, st_ref, o_scr, mix_scr):
    rows = h_ref.shape[0]
    W = HG_WIDTH

    @pl.when(pl.program_id(1) == 0)
    def _():
        st_ref[...] = jnp.zeros_like(st_ref)

    h = h_ref[...]
    proj = _dot(h.astype(BF16), w_in_ref[...])
    q = proj[:, 0:W]
    fz = proj[:, W:2 * W]
    iv = proj[:, 2 * W:3 * W]
    gt = proj[:, 3 * W:4 * W]
    uv = proj[:, 4 * W:]

    lbl = lbl_ref[...]
    lbe = jnp.exp(lbl - jnp.max(lbl, axis=0, keepdims=True))
    lb = lbe[0:1, :] / jnp.sum(lbe, axis=0, keepdims=True)

    e = jnp.exp(-jnp.abs(fz))
    r = 1.0 / (1.0 + e)
    er = e * r
    pos = fz >= 0.0
    lf = jnp.log(lb + (1.0 - lb) * jnp.where(pos, r, er))
    k = (1.0 - lb) * jnp.where(pos, er, r)

    ri = lax.broadcasted_iota(jnp.int32, (rows, rows), 0)
    ci = lax.broadcasted_iota(jnp.int32, (rows, rows), 1)
    tri = ((ri >= ci) & (ri // HG_CHUNK == ci // HG_CHUNK)).astype(BF16)
    lf0 = lf.astype(BF16)
    rem = lf - lf0.astype(F32)
    lf1 = rem.astype(BF16)
    lf2 = (rem - lf1.astype(F32)).astype(BF16)
    b = _dot(tri, lf0) + _dot(tri, lf1) + _dot(tri, lf2)

    rw = lax.broadcasted_iota(jnp.int32, (rows, W), 0)
    vb16 = iv.astype(BF16)

    hi = lax.broadcasted_iota(jnp.int32, (W, W), 0) // HG_DK
    hj = lax.broadcasted_iota(jnp.int32, (W, W), 1) // HG_DK
    head_ones = (hi == hj).astype(BF16)
    sub = rw % HG_SUB
    o_diag = jnp.zeros((rows, W), F32)
    for s in range(HG_SUB):
        kb = _bcast_row(k, HG_SUB, s)
        bb = _bcast_row(b, HG_SUB, s)
        vb = _bcast_row(iv, HG_SUB, s)
        p = jnp.where(sub >= s, q * kb * jnp.exp(jnp.minimum(b - bb, 0.0)), 0.0)
        o_diag = o_diag + _dot(p.astype(BF16), head_ones) * vb

    ql, kl = [], []
    for L in HG_LEVELS:
        el = jnp.exp(-jnp.abs(b - _bcast_row(b, 2 * L, L - 1)))
        upper = (rw % (2 * L)) >= L
        ql.append(jnp.where(upper, q * el, 0.0).astype(BF16))
        kl.append(jnp.where(upper, 0.0, k * el).astype(BF16))
    b_end = _bcast_row(b, HG_CHUNK, HG_CHUNK - 1)
    q_st = (q * jnp.exp(b)).astype(BF16)
    k_st = (k * jnp.exp(b_end - b)).astype(BF16)
    dec_end = jnp.exp(b_end)

    ti = lax.broadcasted_iota(jnp.int32, (HG_CHUNK, HG_CHUNK), 0)
    si = lax.broadcasted_iota(jnp.int32, (HG_CHUNK, HG_CHUNK), 1)
    level_masks = [(ti // (2 * L)) == (si // (2 * L)) for L in HG_LEVELS]

    for hh in range(HG_HEADS):
        cs = slice(hh * HG_DK, (hh + 1) * HG_DK)
        st = st_ref[hh]
        for c in range(rows // HG_CHUNK):
            rs = slice(c * HG_CHUNK, (c + 1) * HG_CHUNK)
            scores = jnp.zeros((HG_CHUNK, HG_CHUNK), F32)
            for li in range(len(HG_LEVELS)):
                sl_ = _dot_nt(ql[li][rs, cs], kl[li][rs, cs])
                scores = scores + jnp.where(level_masks[li], sl_, 0.0)
            v_c = vb16[rs, cs]
            o_c = _dot(scores.astype(BF16), v_c) + _dot_nt(q_st[rs, cs], st.astype(BF16))
            o_scr[rs, cs] = o_c
            last = (c + 1) * HG_CHUNK - 1
            st = st * dec_end[last:last + 1, cs] + _dot_tn(v_c, k_st[rs, cs])
        st_ref[hh] = st

    o = o_scr[...] + o_diag
    hgn = hgn_ref[...]
    for hh in range(HG_HEADS):
        cs = slice(hh * HG_DK, (hh + 1) * HG_DK)
        oh = o[:, cs]
        gh = gt[:, cs]
        ms = jnp.mean(oh * oh, axis=-1, keepdims=True)
        mix_scr[:, cs] = (oh * lax.rsqrt(ms + LN_EPS) * hgn * (gh * _sigmoid(gh))).astype(BF16)

    uvg = uv * (0.5 * (1.0 + jnp.tanh(0.7978845608028654 * (uv + 0.044715 * (uv * uv * uv)))))
    ti = lax.broadcasted_iota(jnp.int32, (SG_CHUNK, SG_CHUNK), 0)
    si = lax.broadcasted_iota(jnp.int32, (SG_CHUNK, SG_CHUNK), 1)
    for g in range(SG_GROUPS):
        cs = slice(g * SG_DIM, (g + 1) * SG_DIM)
        u_g = uvg[:, cs]
        v_g = uvg[:, SG_WIDTH + g * SG_DIM:SG_WIDTH + (g + 1) * SG_DIM]
        vn = _layer_norm(v_g, sglg_ref[:, cs], sglb_ref[:, cs]).astype(BF16)
        w_g = jnp.where(ti >= si, ws_ref[g], 0.0).astype(BF16)
        bias = bst_ref[:, g:g + 1]
        for n in range(rows // SG_CHUNK):
            rs = slice(n * SG_CHUNK, (n + 1) * SG_CHUNK)
            s_ = _dot(w_g, vn[rs, :]) + bias
            mix_scr[rs, HG_WIDTH + g * SG_DIM:HG_WIDTH + (g + 1) * SG_DIM] = (u_g[rs, :] * s_).astype(BF16)

    mix = _dot(mix_scr[...], w_out_ref[...])
    o_ref[...] = _layer_norm(alpha * h + mix, g_ref[...], b_ref[...])


def _xattn_ffn_kernel(alpha, h_ref, k_ref, v_ref, wq_ref, wo_ref, g3_ref, b3_ref,
                      wg_ref, wu_ref, wd_ref, g4_ref, b4_ref, o_ref, a_scr, h_scr):
    h = h_ref[...]
    d_model = h.shape[1]
    hd = d_model // X_HEADS
    qb = (_dot(h.astype(BF16), wq_ref[...]) * (hd ** -0.5)).astype(BF16)
    for hh in range(X_HEADS):
        cs = slice(hh * hd, (hh + 1) * hd)
        s = _dot_nt(qb[:, cs], k_ref[:, cs])
        p = jnp.exp(s - jnp.max(s, axis=-1, keepdims=True))
        l = jnp.sum(p, axis=-1, keepdims=True)
        a_scr[:, cs] = (_dot(p.astype(BF16), v_ref[:, cs]) / l).astype(BF16)
    xa = _dot(a_scr[...], wo_ref[...])
    h3 = _layer_norm(alpha * h + xa, g3_ref[...], b3_ref[...])
    o_ref[...] = _swiglu_ln(h3, alpha, wg_ref, wu_ref, wd_ref, g4_ref, b4_ref, h_scr)


def _resident(shape):
    return pl.BlockSpec(shape, lambda *_: (0,) * len(shape), pipeline_mode=pl.Buffered(1))


def _params(n_grid_axes):
    return pltpu.CompilerParams(dimension_semantics=("arbitrary",) * n_grid_axes,
                                vmem_limit_bytes=VMEM_LIMIT_BYTES)


def kernel(x, mem, ffn1_w_gate, ffn1_w_up, ffn1_w_down, ln1_g, ln1_b, w_in, hg_lb_logits, hg_norm_g, sg_ln_g, sg_ln_b, sg_w_s, sg_b_s, w_out, ln2_g, ln2_b, mem_ln_g, mem_ln_b, xa_w_q, xa_w_k, xa_w_v, xa_w_o, ln3_g, ln3_b, ffn2_w_gate, ffn2_w_up, ffn2_w_down, ln4_g, ln4_b):
    depth = w_in.shape[0]
    assert depth == 1, "single-layer stack"
    B, T, D = x.shape
    n_rows = B * T
    d_ff = ffn1_w_gate.shape[2]
    alpha = (2.0 * depth) ** 0.25
    assert T % MIX_ROWS == 0 and T % FFN_ROWS == 0 and d_ff % FFN_COLS == 0
    assert MIX_ROWS % SG_CHUNK == 0 and MIX_ROWS % HG_CHUNK == 0

    row = lambda a: a.reshape(1, -1)
    bf = lambda a: a.astype(BF16)
    x2 = x.reshape(n_rows, D)
    mem2 = mem.reshape(B * MEM_LEN, D)

    k_mem, v_mem = pl.pallas_call(
        _kv_kernel,
        grid=(B,),
        in_specs=[pl.BlockSpec((MEM_LEN, D), lambda i: (i, 0)),
                  _resident((1, D)), _resident((1, D)), _resident((D, D)), _resident((D, D))],
        out_specs=[pl.BlockSpec((MEM_LEN, D), lambda i: (i, 0))] * 2,
        out_shape=[jax.ShapeDtypeStruct((B * MEM_LEN, D), BF16)] * 2,
        compiler_params=_params(1),
        name="memory_kv",
    )(mem2, row(mem_ln_g[0]), row(mem_ln_b[0]), bf(xa_w_k[0]), bf(xa_w_v[0]))

    row_spec = pl.BlockSpec((FFN_ROWS, D), lambda i: (i, 0))
    h1 = pl.pallas_call(
        functools.partial(_ffn_kernel, alpha),
        grid=(n_rows // FFN_ROWS,),
        in_specs=[row_spec, _resident((D, d_ff)), _resident((D, d_ff)), _resident((d_ff, D)),
                  _resident((1, D)), _resident((1, D))],
        out_specs=row_spec,
        out_shape=jax.ShapeDtypeStruct((n_rows, D), F32),
        scratch_shapes=[pltpu.VMEM((FFN_ROWS, d_ff), BF16)],
        compiler_params=_params(1),
        name="ffn1_ln1",
    )(x2, bf(ffn1_w_gate[0]), bf(ffn1_w_up[0]), bf(ffn1_w_down[0]), row(ln1_g[0]), row(ln1_b[0]))

    tiles_per_seq = T // MIX_ROWS
    in_width = w_in.shape[2]
    mix_spec = pl.BlockSpec((MIX_ROWS, D), lambda bi, ti: (bi * tiles_per_seq + ti, 0))
    h2 = pl.pallas_call(
        functools.partial(_mixer_kernel, alpha),
        grid=(B, tiles_per_seq),
        in_specs=[mix_spec, _resident((D, in_width)),
                  _resident((hg_lb_logits.shape[0], HG_WIDTH)), _resident((1, HG_DK)),
                  _resident((1, SG_WIDTH)), _resident((1, SG_WIDTH)),
                  _resident((SG_GROUPS, SG_CHUNK, SG_CHUNK)), _resident((SG_CHUNK, SG_GROUPS)),
                  _resident((D, D)), _resident((1, D)), _resident((1, D))],
        out_specs=mix_spec,
        out_shape=jax.ShapeDtypeStruct((n_rows, D), F32),
        scratch_shapes=[pltpu.VMEM((HG_HEADS, HG_DK, HG_DK), F32),
                        pltpu.VMEM((MIX_ROWS, HG_WIDTH), F32),
                        pltpu.VMEM((MIX_ROWS, D), BF16)],
        compiler_params=_params(2),
        name="mixer_ln2",
    )(h1, bf(w_in[0]), hg_lb_logits.reshape(hg_lb_logits.shape[0], HG_WIDTH), row(hg_norm_g[0]),
      row(sg_ln_g[0]), row(sg_ln_b[0]), sg_w_s[0], sg_b_s[0].T, bf(w_out[0]),
      row(ln2_g[0]), row(ln2_b[0]))

    tiles_per_batch = T // FFN_ROWS
    kv_spec = pl.BlockSpec((MEM_LEN, D), lambda i: (i // tiles_per_batch, 0))
    out = pl.pallas_call(
        functools.partial(_xattn_ffn_kernel, alpha),
        grid=(n_rows // FFN_ROWS,),
        in_specs=[row_spec, kv_spec, kv_spec, _resident((D, D)), _resident((D, D)),
                  _resident((1, D)), _resident((1, D)),
                  _resident((D, d_ff)), _resident((D, d_ff)), _resident((d_ff, D)),
                  _resident((1, D)), _resident((1, D))],
        out_specs=row_spec,
        out_shape=jax.ShapeDtypeStruct((n_rows, D), F32),
        scratch_shapes=[pltpu.VMEM((FFN_ROWS, D), BF16), pltpu.VMEM((FFN_ROWS, d_ff), BF16)],
        compiler_params=_params(1),
        name="xattn_ln3_ffn2_ln4",
    )(h2, k_mem, v_mem, bf(xa_w_q[0]), bf(xa_w_o[0]), row(ln3_g[0]), row(ln3_b[0]),
      bf(ffn2_w_gate[0]), bf(ffn2_w_up[0]), bf(ffn2_w_down[0]), row(ln4_g[0]), row(ln4_b[0]))

    return out.reshape(B, T, D)
```

```python
import functools

import jax
import jax.numpy as jnp
from jax import lax
from jax.experimental import pallas as pl
from jax.experimental.pallas import tpu as pltpu

F32 = jnp.float32
BF16 = jnp.bfloat16

LN_EPS = 1e-5
MEM_LEN = 256
HG_HEADS = 4
HG_DK = 128
HG_WIDTH = HG_HEADS * HG_DK
SG_GROUPS = 4
SG_DIM = 128
SG_WIDTH = SG_GROUPS * SG_DIM
SG_CHUNK = 128
X_HEADS = 4

HG_CHUNK = 64
HG_LEVELS = (1, 2, 4, 8, 16, 32)

FFN_ROWS = 512
FFN_COLS = 256
MIX_ROWS = 256
VMEM_LIMIT_BYTES = 56 * 1024 * 1024


def _dot(a, b):
    return jnp.dot(a, b, preferred_element_type=F32)


def _dot_nt(a, b):
    return lax.dot_general(a, b, (((1,), (1,)), ((), ())), preferred_element_type=F32)


def _dot_tn(a, b):
    return lax.dot_general(a, b, (((0,), (0,)), ((), ())), preferred_element_type=F32)


def _layer_norm(y, g, b):
    mu = jnp.mean(y, axis=-1, keepdims=True)
    d = y - mu
    var = jnp.mean(d * d, axis=-1, keepdims=True)
    return d * lax.rsqrt(var + LN_EPS) * g + b


def _sigmoid(x):
    return 1.0 / (1.0 + jnp.exp(-x))


def _swiglu_ln(x, alpha, wg_ref, wu_ref, wd_ref, g_ref, b_ref, h_scr):
    xb = x.astype(BF16)
    d_ff = wg_ref.shape[1]
    for j in range(d_ff // FFN_COLS):
        sl = slice(j * FFN_COLS, (j + 1) * FFN_COLS)
        gate = _dot(xb, wg_ref[:, sl])
        up = _dot(xb, wu_ref[:, sl])
        h_scr[:, sl] = (gate * _sigmoid(gate) * up).astype(BF16)
    y = _dot(h_scr[...], wd_ref[...])
    return _layer_norm(alpha * x + 0.5 * y, g_ref[...], b_ref[...])


def _kv_kernel(mem_ref, g_ref, b_ref, wk_ref, wv_ref, k_ref, v_ref):
    m = _layer_norm(mem_ref[...], g_ref[...], b_ref[...]).astype(BF16)
    k_ref[...] = _dot(m, wk_ref[...]).astype(BF16)
    v_ref[...] = _dot(m, wv_ref[...]).astype(BF16)


def _ffn_kernel(alpha, x_ref, wg_ref, wu_ref, wd_ref, g_ref, b_ref, o_ref, h_scr):
    o_ref[...] = _swiglu_ln(x_ref[...], alpha, wg_ref, wu_ref, wd_ref, g_ref, b_ref, h_scr)


def _bcast_row(a, blk, r):
    rows, w = a.shape
    a3 = a.reshape(rows // blk, blk, w)
    return jnp.broadcast_to(a3[:, r:r + 1, :], a3.shape).reshape(rows, w)


def _mixer_kernel(alpha, h_ref, w_in_ref, lbl_ref, hgn_ref, sglg_ref, sglb_ref, ws_ref,
                  bst_ref, w_out_ref, g_ref, b_ref, o_ref, st_ref, o_scr, mix_scr):
    rows = h_ref.shape[0]
    W = HG_WIDTH

    @pl.when(pl.program_id(1) == 0)
    def _():
        st_ref[...] = jnp.zeros_like(st_ref)

    h = h_ref[...]
    proj = _dot(h.astype(BF16), w_in_ref[...])
    q = proj[:, 0:W]
    fz = proj[:, W:2 * W]
    iv = proj[:, 2 * W:3 * W]
    gt = proj[:, 3 * W:4 * W]
    uv = proj[:, 4 * W:]

    lbl = lbl_ref[...]
    lbe = jnp.exp(lbl - jnp.max(lbl, axis=0, keepdims=True))
    lb = lbe[0:1, :] / jnp.sum(lbe, axis=0, keepdims=True)

    e = jnp.exp(-jnp.abs(fz))
    r = 1.0 / (1.0 + e)
    er = e * r
    pos = fz >= 0.0
    f = lb + (1.0 - lb) * jnp.where(pos, r, er)
    k = (1.0 - lb) * jnp.where(pos, er, r)
    lf = jnp.log2(f)

    ri = lax.broadcasted_iota(jnp.int32, (rows, rows), 0)
    ci = lax.broadcasted_iota(jnp.int32, (rows, rows), 1)
    tri = ((ri >= ci) & (ri // HG_CHUNK == ci // HG_CHUNK)).astype(BF16)
    lf0 = lf.astype(BF16)
    rem = lf - lf0.astype(F32)
    lf1 = rem.astype(BF16)
    lf2 = (rem - lf1.astype(F32)).astype(BF16)
    b = _dot(tri, lf0) + _dot(tri, lf1) + _dot(tri, lf2)

    rw = lax.broadcasted_iota(jnp.int32, (rows, W), 0)
    vb16 = iv.astype(BF16)

    hi = lax.broadcasted_iota(jnp.int32, (W, W), 0) // HG_DK
    hj = lax.broadcasted_iota(jnp.int32, (W, W), 1) // HG_DK
    head_ones = (hi == hj).astype(BF16)
    o_diag = _dot((q * k).astype(BF16), head_ones) * iv

    ml = [jnp.where((rw % 2) == 1, q * f, k).astype(BF16)]
    for L in HG_LEVELS[1:]:
        if 2 * L >= 8:
            ref = _bcast_row(b, 2 * L, L - 1)
        else:
            ref = jnp.where((rw % 8) < 4, _bcast_row(b, 8, 1), _bcast_row(b, 8, 5))
        el = jnp.exp2(-jnp.abs(b - ref))
        upper = (rw % (2 * L)) >= L
        ml.append((jnp.where(upper, q, k) * el).astype(BF16))
    b_end = _bcast_row(b, HG_CHUNK, HG_CHUNK - 1)
    q_st = (q * jnp.exp2(b)).astype(BF16)
    k_st = (k * jnp.exp2(b_end - b)).astype(BF16)
    dec_end = jnp.exp2(b_end)

    ti = lax.broadcasted_iota(jnp.int32, (HG_CHUNK, HG_CHUNK), 0)
    si = lax.broadcasted_iota(jnp.int32, (HG_CHUNK, HG_CHUNK), 1)
    level_masks = [((ti // (2 * L)) == (si // (2 * L))) & ((ti % (2 * L)) >= L) & ((si % (2 * L)) < L)
                   for L in HG_LEVELS]

    n_chunks = rows // HG_CHUNK
    blocks = [(hh, c) for hh in range(HG_HEADS) for c in range(n_chunks)]
    col = lambda hh: slice(hh * HG_DK, (hh + 1) * HG_DK)
    row_ = lambda c: slice(c * HG_CHUNK, (c + 1) * HG_CHUNK)
    scores, st_inc = {}, {}
    for hh, c in blocks:
        s_ = jnp.zeros((HG_CHUNK, HG_CHUNK), F32)
        for li in range(len(HG_LEVELS)):
            m_c = ml[li][row_(c), col(hh)]
            s_ = s_ + jnp.where(level_masks[li], _dot_nt(m_c, m_c), 0.0)
        scores[hh, c] = s_.astype(BF16)
        st_inc[hh, c] = _dot_tn(vb16[row_(c), col(hh)], k_st[row_(c), col(hh)])
    for hh in range(HG_HEADS):
        st = st_ref[hh]
        for c in range(n_chunks):
            o_scr[row_(c), col(hh)] = (_dot(scores[hh, c], vb16[row_(c), col(hh)])
                                       + _dot_nt(q_st[row_(c), col(hh)], st.astype(BF16)))
            last = (c + 1) * HG_CHUNK - 1
            st = st * dec_end[last:last + 1, col(hh)] + st_inc[hh, c]
        st_ref[hh] = st

    o = o_scr[...] + o_diag
    hgn = hgn_ref[...]
    for hh in range(HG_HEADS):
        cs = slice(hh * HG_DK, (hh + 1) * HG_DK)
        oh = o[:, cs]
        gh = gt[:, cs]
        ms = jnp.mean(oh * oh, axis=-1, keepdims=True)
        mix_scr[:, cs] = (oh * lax.rsqrt(ms + LN_EPS) * hgn * (gh * _sigmoid(gh))).astype(BF16)

    uvg = uv * (0.5 * (1.0 + jnp.tanh(0.7978845608028654 * (uv + 0.044715 * (uv * uv * uv)))))
    ti = lax.broadcasted_iota(jnp.int32, (SG_CHUNK, SG_CHUNK), 0)
    si = lax.broadcasted_iota(jnp.int32, (SG_CHUNK, SG_CHUNK), 1)
    for g in range(SG_GROUPS):
        cs = slice(g * SG_DIM, (g + 1) * SG_DIM)
        u_g = uvg[:, cs]
        v_g = uvg[:, SG_WIDTH + g * SG_DIM:SG_WIDTH + (g + 1) * SG_DIM]
        vn = _layer_norm(v_g, sglg_ref[:, cs], sglb_ref[:, cs]).astype(BF16)
        w_g = jnp.where(ti >= si, ws_ref[g], 0.0).astype(BF16)
        bias = bst_ref[:, g:g + 1]
        for n in range(rows // SG_CHUNK):
            rs = slice(n * SG_CHUNK, (n + 1) * SG_CHUNK)
            s_ = _dot(w_g, vn[rs, :]) + bias
            mix_scr[rs, HG_WIDTH + g * SG_DIM:HG_WIDTH + (g + 1) * SG_DIM] = (u_g[rs, :] * s_).astype(BF16)

    mix = _dot(mix_scr[...], w_out_ref[...])
    o_ref[...] = _layer_norm(alpha * h + mix, g_ref[...], b_ref[...])


def _xattn_ffn_kernel(alpha, h_ref, k_ref, v_ref, wq_ref, wo_ref, g3_ref, b3_ref,
                      wg_ref, wu_ref, wd_ref, g4_ref, b4_ref, o_ref, a_scr, h_scr):
    h = h_ref[...]
    d_model = h.shape[1]
    hd = d_model // X_HEADS
    qb = (_dot(h.astype(BF16), wq_ref[...]) * (hd ** -0.5)).astype(BF16)
    for hh in range(X_HEADS):
        cs = slice(hh * hd, (hh + 1) * hd)
        s = _dot_nt(qb[:, cs], k_ref[:, cs])
        p = jnp.exp(s - jnp.max(s, axis=-1, keepdims=True))
        l = jnp.sum(p, axis=-1, keepdims=True)
        a_scr[:, cs] = (_dot(p.astype(BF16), v_ref[:, cs]) / l).astype(BF16)
    xa = _dot(a_scr[...], wo_ref[...])
    h3 = _layer_norm(alpha * h + xa, g3_ref[...], b3_ref[...])
    o_ref[...] = _swiglu_ln(h3, alpha, wg_ref, wu_ref, wd_ref, g4_ref, b4_ref, h_scr)


def _resident(shape):
    return pl.BlockSpec(shape, lambda *_: (0,) * len(shape), pipeline_mode=pl.Buffered(1))


def _params(n_grid_axes):
    return pltpu.CompilerParams(dimension_semantics=("arbitrary",) * n_grid_axes,
                                vmem_limit_bytes=VMEM_LIMIT_BYTES)


def kernel(x, mem, ffn1_w_gate, ffn1_w_up, ffn1_w_down, ln1_g, ln1_b, w_in, hg_lb_logits, hg_norm_g, sg_ln_g, sg_ln_b, sg_w_s, sg_b_s, w_out, ln2_g, ln2_b, mem_ln_g, mem_ln_b, xa_w_q, xa_w_k, xa_w_v, xa_w_o, ln3_g, ln3_b, ffn2_w_gate, ffn2_w_up, ffn2_w_down, ln4_g, ln4_b):
    depth = w_in.shape[0]
    assert depth == 1, "single-layer stack"
    B, T, D = x.shape
    n_rows = B * T
    d_ff = ffn1_w_gate.shape[2]
    alpha = (2.0 * depth) ** 0.25
    assert T % MIX_ROWS == 0 and T % FFN_ROWS == 0 and d_ff % FFN_COLS == 0
    assert MIX_ROWS % SG_CHUNK == 0 and MIX_ROWS % HG_CHUNK == 0

    row = lambda a: a.reshape(1, -1)
    bf = lambda a: a.astype(BF16)
    x2 = x.reshape(n_rows, D)
    mem2 = mem.reshape(B * MEM_LEN, D)

    k_mem, v_mem = pl.pallas_call(
        _kv_kernel,
        grid=(B,),
        in_specs=[pl.BlockSpec((MEM_LEN, D), lambda i: (i, 0)),
                  _resident((1, D)), _resident((1, D)), _resident((D, D)), _resident((D, D))],
        out_specs=[pl.BlockSpec((MEM_LEN, D), lambda i: (i, 0))] * 2,
        out_shape=[jax.ShapeDtypeStruct((B * MEM_LEN, D), BF16)] * 2,
        compiler_params=_params(1),
        name="memory_kv",
    )(mem2, row(mem_ln_g[0]), row(mem_ln_b[0]), bf(xa_w_k[0]), bf(xa_w_v[0]))

    row_spec = pl.BlockSpec((FFN_ROWS, D), lambda i: (i, 0))
    h1 = pl.pallas_call(
        functools.partial(_ffn_kernel, alpha),
        grid=(n_rows // FFN_ROWS,),
        in_specs=[row_spec, _resident((D, d_ff)), _resident((D, d_ff)), _resident((d_ff, D)),
                  _resident((1, D)), _resident((1, D))],
        out_specs=row_spec,
        out_shape=jax.ShapeDtypeStruct((n_rows, D), F32),
        scratch_shapes=[pltpu.VMEM((FFN_ROWS, d_ff), BF16)],
        compiler_params=_params(1),
        name="ffn1_ln1",
    )(x2, bf(ffn1_w_gate[0]), bf(ffn1_w_up[0]), bf(ffn1_w_down[0]), row(ln1_g[0]), row(ln1_b[0]))

    tiles_per_seq = T // MIX_ROWS
    in_width = w_in.shape[2]
    mix_spec = pl.BlockSpec((MIX_ROWS, D), lambda bi, ti: (bi * tiles_per_seq + ti, 0))
    h2 = pl.pallas_call(
        functools.partial(_mixer_kernel, alpha),
        grid=(B, tiles_per_seq),
        in_specs=[mix_spec, _resident((D, in_width)),
                  _resident((hg_lb_logits.shape[0], HG_WIDTH)), _resident((1, HG_DK)),
                  _resident((1, SG_WIDTH)), _resident((1, SG_WIDTH)),
                  _resident((SG_GROUPS, SG_CHUNK, SG_CHUNK)), _resident((SG_CHUNK, SG_GROUPS)),
                  _resident((D, D)), _resident((1, D)), _resident((1, D))],
        out_specs=mix_spec,
        out_shape=jax.ShapeDtypeStruct((n_rows, D), F32),
        scratch_shapes=[pltpu.VMEM((HG_HEADS, HG_DK, HG_DK), F32),
                        pltpu.VMEM((MIX_ROWS, HG_WIDTH), F32),
                        pltpu.VMEM((MIX_ROWS, D), BF16)],
        compiler_params=_params(2),
        name="mixer_ln2",
    )(h1, bf(w_in[0]), hg_lb_logits.reshape(hg_lb_logits.shape[0], HG_WIDTH), row(hg_norm_g[0]),
      row(sg_ln_g[0]), row(sg_ln_b[0]), sg_w_s[0], sg_b_s[0].T, bf(w_out[0]),
      row(ln2_g[0]), row(ln2_b[0]))

    tiles_per_batch = T // FFN_ROWS
    kv_spec = pl.BlockSpec((MEM_LEN, D), lambda i: (i // tiles_per_batch, 0))
    out = pl.pallas_call(
        functools.partial(_xattn_ffn_kernel, alpha),
        grid=(n_rows // FFN_ROWS,),
        in_specs=[row_spec, kv_spec, kv_spec, _resident((D, D)), _resident((D, D)),
                  _resident((1, D)), _resident((1, D)),
                  _resident((D, d_ff)), _resident((D, d_ff)), _resident((d_ff, D)),
                  _resident((1, D)), _resident((1, D))],
        out_specs=row_spec,
        out_shape=jax.ShapeDtypeStruct((n_rows, D), F32),
        scratch_shapes=[pltpu.VMEM((FFN_ROWS, D), BF16), pltpu.VMEM((FFN_ROWS, d_ff), BF16)],
        compiler_params=_params(1),
        name="xattn_ln3_ffn2_ln4",
    )(h2, k_mem, v_mem, bf(xa_w_q[0]), bf(xa_w_o[0]), row(ln3_g[0]), row(ln3_b[0]),
      bf(ffn2_w_gate[0]), bf(ffn2_w_up[0]), bf(ffn2_w_down[0]), row(ln4_g[0]), row(ln4_b[0]))

    return out.reshape(B, T, D)
```

```python
import functools

import jax
import jax.numpy as jnp
from jax import lax
from jax.experimental import pallas as pl
from jax.experimental.pallas import tpu as pltpu

F32 = jnp.float32
BF16 = jnp.bfloat16

LN_EPS = 1e-5
MEM_LEN = 256
HG_HEADS = 4
HG_DK = 128
HG_WIDTH = HG_HEADS * HG_DK
SG_GROUPS = 4
SG_DIM = 128
SG_WIDTH = SG_GROUPS * SG_DIM
SG_CHUNK = 128
X_HEADS = 4

HG_CHUNK = 64
HG_LEVELS = (1, 2, 4, 8, 16, 32)

ROWS = 256
FFN_COLS = 256
VMEM_LIMIT_BYTES = 56 * 1024 * 1024


def _dot(a, b):
    return jnp.dot(a, b, preferred_element_type=F32)


def _dot_nt(a, b):
    return lax.dot_general(a, b, (((1,), (1,)), ((), ())), preferred_element_type=F32)


def _dot_tn(a, b):
    return lax.dot_general(a, b, (((0,), (0,)), ((), ())), preferred_element_type=F32)


def _layer_norm(y, g, b):
    mu = jnp.mean(y, axis=-1, keepdims=True)
    d = y - mu
    var = jnp.mean(d * d, axis=-1, keepdims=True)
    return d * lax.rsqrt(var + LN_EPS) * g + b


def _sigmoid(x):
    return 1.0 / (1.0 + jnp.exp(-x))


def _swiglu_ln(x, alpha, wg_ref, wu_ref, wd_ref, g_ref, b_ref, h_scr):
    xb = x.astype(BF16)
    d_ff = wg_ref.shape[1]
    for j in range(d_ff // FFN_COLS):
        sl = slice(j * FFN_COLS, (j + 1) * FFN_COLS)
        gate = _dot(xb, wg_ref[:, sl])
        up = _dot(xb, wu_ref[:, sl])
        h_scr[:, sl] = (gate * _sigmoid(gate) * up).astype(BF16)
    y = _dot(h_scr[...], wd_ref[...])
    return _layer_norm(alpha * x + 0.5 * y, g_ref[...], b_ref[...])


def _kv_kernel(mem_ref, g_ref, b_ref, wk_ref, wv_ref, k_ref, v_ref):
    m = _layer_norm(mem_ref[...], g_ref[...], b_ref[...]).astype(BF16)
    k_ref[...] = _dot(m, wk_ref[...]).astype(BF16)
    v_ref[...] = _dot(m, wv_ref[...]).astype(BF16)


def _bcast_row(a, blk, r):
    rows, w = a.shape
    a3 = a.reshape(rows // blk, blk, w)
    return jnp.broadcast_to(a3[:, r:r + 1, :], a3.shape).reshape(rows, w)


def _token_mixers(proj_ref, first_of_seq, lbl_ref, hgn_ref, sglg_ref, sglb_ref, ws_ref, bst_ref,
                  mix_ref, st_ref, o_scr):
    rows = proj_ref.shape[0]
    W = HG_WIDTH
    q = proj_ref[:, 0:W]
    fz = proj_ref[:, W:2 * W]
    iv = proj_ref[:, 2 * W:3 * W]
    gt = proj_ref[:, 3 * W:4 * W]
    uv = proj_ref[:, 4 * W:]

    lbl = lbl_ref[...]
    lbe = jnp.exp(lbl - jnp.max(lbl, axis=0, keepdims=True))
    lb = lbe[0:1, :] / jnp.sum(lbe, axis=0, keepdims=True)

    e = jnp.exp(-jnp.abs(fz))
    r = 1.0 / (1.0 + e)
    er = e * r
    pos = fz >= 0.0
    f = lb + (1.0 - lb) * jnp.where(pos, r, er)
    k = (1.0 - lb) * jnp.where(pos, er, r)
    lf = jnp.log2(f)

    ri = lax.broadcasted_iota(jnp.int32, (rows, rows), 0)
    ci = lax.broadcasted_iota(jnp.int32, (rows, rows), 1)
    tri = ((ri >= ci) & (ri // HG_CHUNK == ci // HG_CHUNK)).astype(BF16)
    lf0 = lf.astype(BF16)
    rem = lf - lf0.astype(F32)
    lf1 = rem.astype(BF16)
    lf2 = (rem - lf1.astype(F32)).astype(BF16)
    b = _dot(tri, lf0) + _dot(tri, lf1) + _dot(tri, lf2)

    rw = lax.broadcasted_iota(jnp.int32, (rows, W), 0)
    vb16 = iv.astype(BF16)

    hi = lax.broadcasted_iota(jnp.int32, (W, W), 0) // HG_DK
    hj = lax.broadcasted_iota(jnp.int32, (W, W), 1) // HG_DK
    head_ones = (hi == hj).astype(BF16)
    o_diag = _dot((q * k).astype(BF16), head_ones) * iv

    ml = [jnp.where((rw % 2) == 1, q * f, k).astype(BF16)]
    for L in HG_LEVELS[1:]:
        if 2 * L >= 8:
            ref = _bcast_row(b, 2 * L, L - 1)
        else:
            ref = jnp.where((rw % 8) < 4, _bcast_row(b, 8, 1), _bcast_row(b, 8, 5))
        el = jnp.exp2(-jnp.abs(b - ref))
        upper = (rw % (2 * L)) >= L
        ml.append((jnp.where(upper, q, k) * el).astype(BF16))
    b_end = _bcast_row(b, HG_CHUNK, HG_CHUNK - 1)
    q_st = (q * jnp.exp2(b)).astype(BF16)
    k_st = (k * jnp.exp2(b_end - b)).astype(BF16)
    dec_end = jnp.exp2(b_end)

    ti = lax.broadcasted_iota(jnp.int32, (HG_CHUNK, HG_CHUNK), 0)
    si = lax.broadcasted_iota(jnp.int32, (HG_CHUNK, HG_CHUNK), 1)
    level_masks = [((ti // (2 * L)) == (si // (2 * L))) & ((ti % (2 * L)) >= L) & ((si % (2 * L)) < L)
                   for L in HG_LEVELS]

    n_chunks = rows // HG_CHUNK
    blocks = [(hh, c) for hh in range(HG_HEADS) for c in range(n_chunks)]
    col = lambda hh: slice(hh * HG_DK, (hh + 1) * HG_DK)
    row_ = lambda c: slice(c * HG_CHUNK, (c + 1) * HG_CHUNK)
    scores, st_inc = {}, {}
    for hh, c in blocks:
        s_ = jnp.zeros((HG_CHUNK, HG_CHUNK), F32)
        for li in range(len(HG_LEVELS)):
            m_c = ml[li][row_(c), col(hh)]
            s_ = s_ + jnp.where(level_masks[li], _dot_nt(m_c, m_c), 0.0)
        scores[hh, c] = s_.astype(BF16)
        st_inc[hh, c] = _dot_tn(vb16[row_(c), col(hh)], k_st[row_(c), col(hh)])
    for hh in range(HG_HEADS):
        st = jnp.where(first_of_seq, 0.0, st_ref[hh])
        for c in range(n_chunks):
            o_scr[row_(c), col(hh)] = (_dot(scores[hh, c], vb16[row_(c), col(hh)])
                                       + _dot_nt(q_st[row_(c), col(hh)], st.astype(BF16)))
            last = (c + 1) * HG_CHUNK - 1
            st = st * dec_end[last:last + 1, col(hh)] + st_inc[hh, c]
        st_ref[hh] = st

    o = o_scr[...] + o_diag
    hgn = hgn_ref[...]
    for hh in range(HG_HEADS):
        oh = o[:, col(hh)]
        gh = gt[:, col(hh)]
        ms = jnp.mean(oh * oh, axis=-1, keepdims=True)
        mix_ref[:, col(hh)] = (oh * lax.rsqrt(ms + LN_EPS) * hgn * (gh * _sigmoid(gh))).astype(BF16)

    uvg = uv * (0.5 * (1.0 + jnp.tanh(0.7978845608028654 * (uv + 0.044715 * (uv * uv * uv)))))
    ti = lax.broadcasted_iota(jnp.int32, (SG_CHUNK, SG_CHUNK), 0)
    si = lax.broadcasted_iota(jnp.int32, (SG_CHUNK, SG_CHUNK), 1)
    for g in range(SG_GROUPS):
        cs = slice(g * SG_DIM, (g + 1) * SG_DIM)
        u_g = uvg[:, cs]
        v_g = uvg[:, SG_WIDTH + g * SG_DIM:SG_WIDTH + (g + 1) * SG_DIM]
        vn = _layer_norm(v_g, sglg_ref[:, cs], sglb_ref[:, cs]).astype(BF16)
        w_g = jnp.where(ti >= si, ws_ref[g], 0.0).astype(BF16)
        bias = bst_ref[:, g:g + 1]
        for n in range(rows // SG_CHUNK):
            rs = slice(n * SG_CHUNK, (n + 1) * SG_CHUNK)
            s_ = _dot(w_g, vn[rs, :]) + bias
            mix_ref[rs, HG_WIDTH + g * SG_DIM:HG_WIDTH + (g + 1) * SG_DIM] = (u_g[rs, :] * s_).astype(BF16)


def _ffn_mixer_kernel(alpha, tiles_per_seq, x_ref, wg_ref, wu_ref, wd_ref, g1_ref, b1_ref, w_in_ref,
                      lbl_ref, hgn_ref, sglg_ref, sglb_ref, ws_ref, bst_ref,
                      h1_ref, mix_ref, h_scr, proj_scr, st_ref, o_scr):
    s = pl.program_id(0)

    @pl.when(s == 0)
    def _():
        proj_scr[...] = jnp.zeros_like(proj_scr)
        st_ref[...] = jnp.zeros_like(st_ref)

    first_of_seq = ((s - 1) % tiles_per_seq) == 0
    _token_mixers(proj_scr, first_of_seq, lbl_ref, hgn_ref, sglg_ref, sglb_ref, ws_ref, bst_ref,
                  mix_ref, st_ref, o_scr)

    h1 = _swiglu_ln(x_ref[...], alpha, wg_ref, wu_ref, wd_ref, g1_ref, b1_ref, h_scr)
    h1_ref[...] = h1
    proj_scr[...] = _dot(h1.astype(BF16), w_in_ref[...])


def _out_xattn_ffn_kernel(alpha, h_ref, mix_ref, k_ref, v_ref, w_out_ref, g2_ref, b2_ref, wq_ref, wo_ref,
                          g3_ref, b3_ref, wg_ref, wu_ref, wd_ref, g4_ref, b4_ref, o_ref, a_scr, h_scr):
    h2 = _layer_norm(alpha * h_ref[...] + _dot(mix_ref[...], w_out_ref[...]), g2_ref[...], b2_ref[...])
    d_model = h2.shape[1]
    hd = d_model // X_HEADS
    qb = (_dot(h2.astype(BF16), wq_ref[...]) * (hd ** -0.5)).astype(BF16)
    for hh in range(X_HEADS):
        cs = slice(hh * hd, (hh + 1) * hd)
        s = _dot_nt(qb[:, cs], k_ref[:, cs])
        p = jnp.exp(s - jnp.max(s, axis=-1, keepdims=True))
        l = jnp.sum(p, axis=-1, keepdims=True)
        a_scr[:, cs] = (_dot(p.astype(BF16), v_ref[:, cs]) / l).astype(BF16)
    xa = _dot(a_scr[...], wo_ref[...])
    h3 = _layer_norm(alpha * h2 + xa, g3_ref[...], b3_ref[...])
    o_ref[...] = _swiglu_ln(h3, alpha, wg_ref, wu_ref, wd_ref, g4_ref, b4_ref, h_scr)


def _resident(shape):
    return pl.BlockSpec(shape, lambda *_: (0,) * len(shape), pipeline_mode=pl.Buffered(1))


def _params():
    return pltpu.CompilerParams(dimension_semantics=("arbitrary",), vmem_limit_bytes=VMEM_LIMIT_BYTES)


def kernel(x, mem, ffn1_w_gate, ffn1_w_up, ffn1_w_down, ln1_g, ln1_b, w_in, hg_lb_logits, hg_norm_g, sg_ln_g, sg_ln_b, sg_w_s, sg_b_s, w_out, ln2_g, ln2_b, mem_ln_g, mem_ln_b, xa_w_q, xa_w_k, xa_w_v, xa_w_o, ln3_g, ln3_b, ffn2_w_gate, ffn2_w_up, ffn2_w_down, ln4_g, ln4_b):
    depth = w_in.shape[0]
    assert depth == 1, "single-layer stack"
    B, T, D = x.shape
    n_rows = B * T
    d_ff = ffn1_w_gate.shape[2]
    in_width = w_in.shape[2]
    alpha = (2.0 * depth) ** 0.25
    assert T % ROWS == 0 and d_ff % FFN_COLS == 0
    assert ROWS % SG_CHUNK == 0 and ROWS % HG_CHUNK == 0
    n_tiles = n_rows // ROWS
    tiles_per_seq = T // ROWS

    row = lambda a: a.reshape(1, -1)
    bf = lambda a: a.astype(BF16)
    x2 = x.reshape(n_rows, D)
    mem2 = mem.reshape(B * MEM_LEN, D)

    k_mem, v_mem = pl.pallas_call(
        _kv_kernel,
        grid=(B,),
        in_specs=[pl.BlockSpec((MEM_LEN, D), lambda i: (i, 0)),
                  _resident((1, D)), _resident((1, D)), _resident((D, D)), _resident((D, D))],
        out_specs=[pl.BlockSpec((MEM_LEN, D), lambda i: (i, 0))] * 2,
        out_shape=[jax.ShapeDtypeStruct((B * MEM_LEN, D), BF16)] * 2,
        compiler_params=_params(),
        name="memory_kv",
    )(mem2, row(mem_ln_g[0]), row(mem_ln_b[0]), bf(xa_w_k[0]), bf(xa_w_v[0]))

    cur_tile = pl.BlockSpec((ROWS, D), lambda s: (jnp.minimum(s, n_tiles - 1), 0))
    prev_tile = pl.BlockSpec((ROWS, D), lambda s: (jnp.maximum(s - 1, 0), 0))
    h1, mix = pl.pallas_call(
        functools.partial(_ffn_mixer_kernel, alpha, tiles_per_seq),
        grid=(n_tiles + 1,),
        in_specs=[cur_tile, _resident((D, d_ff)), _resident((D, d_ff)), _resident((d_ff, D)),
                  _resident((1, D)), _resident((1, D)), _resident((D, in_width)),
                  _resident((hg_lb_logits.shape[0], HG_WIDTH)), _resident((1, HG_DK)),
                  _resident((1, SG_WIDTH)), _resident((1, SG_WIDTH)),
                  _resident((SG_GROUPS, SG_CHUNK, SG_CHUNK)), _resident((SG_CHUNK, SG_GROUPS))],
        out_specs=[cur_tile, prev_tile],
        out_shape=[jax.ShapeDtypeStruct((n_rows, D), F32), jax.ShapeDtypeStruct((n_rows, D), BF16)],
        scratch_shapes=[pltpu.VMEM((ROWS, d_ff), BF16),
                        pltpu.VMEM((ROWS, in_width), F32),
                        pltpu.VMEM((HG_HEADS, HG_DK, HG_DK), F32),
                        pltpu.VMEM((ROWS, HG_WIDTH), F32)],
        compiler_params=_params(),
        name="ffn1_ln1_mixers",
    )(x2, bf(ffn1_w_gate[0]), bf(ffn1_w_up[0]), bf(ffn1_w_down[0]), row(ln1_g[0]), row(ln1_b[0]),
      bf(w_in[0]), hg_lb_logits.reshape(hg_lb_logits.shape[0], HG_WIDTH), row(hg_norm_g[0]),
      row(sg_ln_g[0]), row(sg_ln_b[0]), sg_w_s[0], sg_b_s[0].T)

    tile = pl.BlockSpec((ROWS, D), lambda i: (i, 0))
    kv_spec = pl.BlockSpec((MEM_LEN, D), lambda i: (i // tiles_per_seq, 0))
    out = pl.pallas_call(
        functools.partial(_out_xattn_ffn_kernel, alpha),
        grid=(n_tiles,),
        in_specs=[tile, tile, kv_spec, kv_spec, _resident((D, D)), _resident((1, D)), _resident((1, D)),
                  _resident((D, D)), _resident((D, D)), _resident((1, D)), _resident((1, D)),
                  _resident((D, d_ff)), _resident((D, d_ff)), _resident((d_ff, D)),
                  _resident((1, D)), _resident((1, D))],
        out_specs=tile,
        out_shape=jax.ShapeDtypeStruct((n_rows, D), F32),
        scratch_shapes=[pltpu.VMEM((ROWS, D), BF16), pltpu.VMEM((ROWS, d_ff), BF16)],
        compiler_params=_params(),
        name="outproj_ln2_xattn_ln3_ffn2_ln4",
    )(h1, mix, k_mem, v_mem, bf(w_out[0]), row(ln2_g[0]), row(ln2_b[0]), bf(xa_w_q[0]), bf(xa_w_o[0]),
      row(ln3_g[0]), row(ln3_b[0]), bf(ffn2_w_gate[0]), bf(ffn2_w_up[0]), bf(ffn2_w_down[0]),
      row(ln4_g[0]), row(ln4_b[0]))

    return out.reshape(B, T, D)
```

```python
import functools

import jax
import jax.numpy as jnp
from jax import lax
from jax.experimental import pallas as pl
from jax.experimental.pallas import tpu as pltpu

F32 = jnp.float32
BF16 = jnp.bfloat16

LN_EPS = 1e-5
MEM_LEN = 256
HG_HEADS = 4
HG_DK = 128
HG_WIDTH = HG_HEADS * HG_DK
SG_GROUPS = 4
SG_DIM = 128
SG_WIDTH = SG_GROUPS * SG_DIM
SG_CHUNK = 128
X_HEADS = 4

HG_CHUNK = 64
HG_LEVELS = (1, 2, 4, 8, 16, 32)
HG_SAFE_LOG2 = 100.0

FFN_ROWS = 512
FFN_COLS = 256
MIX_ROWS = 256
VMEM_LIMIT_BYTES = 56 * 1024 * 1024


def _dot(a, b):
    return jnp.dot(a, b, preferred_element_type=F32)


def _dot_nt(a, b):
    return lax.dot_general(a, b, (((1,), (1,)), ((), ())), preferred_element_type=F32)


def _dot_tn(a, b):
    return lax.dot_general(a, b, (((0,), (0,)), ((), ())), preferred_element_type=F32)


def _layer_norm(y, g, b):
    mu = jnp.mean(y, axis=-1, keepdims=True)
    d = y - mu
    var = jnp.mean(d * d, axis=-1, keepdims=True)
    return d * lax.rsqrt(var + LN_EPS) * g + b


def _sigmoid(x):
    return 1.0 / (1.0 + jnp.exp(-x))


def _swiglu_ln(x, alpha, wg_ref, wu_ref, wd_ref, g_ref, b_ref, h_scr):
    xb = x.astype(BF16)
    d_ff = wg_ref.shape[1]
    for j in range(d_ff // FFN_COLS):
        sl = slice(j * FFN_COLS, (j + 1) * FFN_COLS)
        gate = _dot(xb, wg_ref[:, sl])
        up = _dot(xb, wu_ref[:, sl])
        h_scr[:, sl] = (gate * _sigmoid(gate) * up).astype(BF16)
    y = _dot(h_scr[...], wd_ref[...])
    return _layer_norm(alpha * x + 0.5 * y, g_ref[...], b_ref[...])


def _kv_kernel(mem_ref, g_ref, b_ref, wk_ref, wv_ref, k_ref, v_ref):
    m = _layer_norm(mem_ref[...], g_ref[...], b_ref[...]).astype(BF16)
    k_ref[...] = _dot(m, wk_ref[...]).astype(BF16)
    v_ref[...] = _dot(m, wv_ref[...]).astype(BF16)


def _ffn_kernel(alpha, x_ref, wg_ref, wu_ref, wd_ref, g_ref, b_ref, o_ref, h_scr):
    o_ref[...] = _swiglu_ln(x_ref[...], alpha, wg_ref, wu_ref, wd_ref, g_ref, b_ref, h_scr)


def _bcast_row(a, blk, r):
    rows, w = a.shape
    a3 = a.reshape(rows // blk, blk, w)
    return jnp.broadcast_to(a3[:, r:r + 1, :], a3.shape).reshape(rows, w)


def _mixer_kernel(alpha, h_ref, w_in_ref, lbl_ref, hgn_ref, sglg_ref, sglb_ref, ws_ref,
                  bst_ref, w_out_ref, g_ref, b_ref, o_ref, st_ref, o_scr, mix_scr, sc_scr):
    rows = h_ref.shape[0]
    W = HG_WIDTH

    @pl.when(pl.program_id(1) == 0)
    def _():
        st_ref[...] = jnp.zeros_like(st_ref)

    h = h_ref[...]
    proj = _dot(h.astype(BF16), w_in_ref[...])
    q = proj[:, 0:W]
    fz = proj[:, W:2 * W]
    iv = proj[:, 2 * W:3 * W]
    gt = proj[:, 3 * W:4 * W]
    uv = proj[:, 4 * W:]

    lbl = lbl_ref[...]
    lbe = jnp.exp(lbl - jnp.max(lbl, axis=0, keepdims=True))
    lb = lbe[0:1, :] / jnp.sum(lbe, axis=0, keepdims=True)

    e = jnp.exp(-jnp.abs(fz))
    r = 1.0 / (1.0 + e)
    er = e * r
    pos = fz >= 0.0
    f = lb + (1.0 - lb) * jnp.where(pos, r, er)
    k = (1.0 - lb) * jnp.where(pos, er, r)
    lf = jnp.log2(f)

    ri = lax.broadcasted_iota(jnp.int32, (rows, rows), 0)
    ci = lax.broadcasted_iota(jnp.int32, (rows, rows), 1)
    tri = ((ri >= ci) & (ri // HG_CHUNK == ci // HG_CHUNK)).astype(BF16)
    lf0 = lf.astype(BF16)
    rem = lf - lf0.astype(F32)
    lf1 = rem.astype(BF16)
    lf2 = (rem - lf1.astype(F32)).astype(BF16)
    b = _dot(tri, lf0) + _dot(tri, lf1) + _dot(tri, lf2)

    rw = lax.broadcasted_iota(jnp.int32, (rows, W), 0)
    vb16 = iv.astype(BF16)
    n_chunks = rows // HG_CHUNK
    blocks = [(hh, c) for hh in range(HG_HEADS) for c in range(n_chunks)]
    col = lambda hh: slice(hh * HG_DK, (hh + 1) * HG_DK)
    row_ = lambda c: slice(c * HG_CHUNK, (c + 1) * HG_CHUNK)
    ti = lax.broadcasted_iota(jnp.int32, (HG_CHUNK, HG_CHUNK), 0)
    si = lax.broadcasted_iota(jnp.int32, (HG_CHUNK, HG_CHUNK), 1)

    b_end = _bcast_row(b, HG_CHUNK, HG_CHUNK - 1)
    q_st = (q * jnp.exp2(b)).astype(BF16)
    k_st = (k * jnp.exp2(b_end - b)).astype(BF16)
    dec_end = jnp.exp2(b_end)
    st_inc = {(hh, c): _dot_tn(vb16[row_(c), col(hh)], k_st[row_(c), col(hh)]) for hh, c in blocks}
    gate_act = gt * _sigmoid(gt)

    uvg = uv * (0.5 * (1.0 + jnp.tanh(0.7978845608028654 * (uv + 0.044715 * (uv * uv * uv)))))
    causal = (lax.broadcasted_iota(jnp.int32, (SG_CHUNK, SG_CHUNK), 0)
              >= lax.broadcasted_iota(jnp.int32, (SG_CHUNK, SG_CHUNK), 1))
    for g in range(SG_GROUPS):
        cs = slice(g * SG_DIM, (g + 1) * SG_DIM)
        u_g = uvg[:, cs]
        v_g = uvg[:, SG_WIDTH + g * SG_DIM:SG_WIDTH + (g + 1) * SG_DIM]
        vn = _layer_norm(v_g, sglg_ref[:, cs], sglb_ref[:, cs]).astype(BF16)
        w_g = jnp.where(causal, ws_ref[g], 0.0).astype(BF16)
        bias = bst_ref[:, g:g + 1]
        for n in range(rows // SG_CHUNK):
            rs = slice(n * SG_CHUNK, (n + 1) * SG_CHUNK)
            s_ = _dot(w_g, vn[rs, :]) + bias
            mix_scr[rs, HG_WIDTH + g * SG_DIM:HG_WIDTH + (g + 1) * SG_DIM] = (u_g[rs, :] * s_).astype(BF16)

    safe =jnp.max(-b_end) < HG_SAFE_LOG2

    @pl.when(safe)
    def _():
        x = b - _bcast_row(b, HG_CHUNK, HG_CHUNK // 2 - 1)
        qm = (q * jnp.exp2(x)).astype(BF16)
        km = (k * jnp.exp2(-x)).astype(BF16)
        for i, (hh, c) in enumerate(blocks):
            s_ = _dot_nt(qm[row_(c), col(hh)], km[row_(c), col(hh)])
            sc_scr[i] = jnp.where(ti >= si, s_, 0.0).astype(BF16)

    @pl.when(jnp.logical_not(safe))
    def _():
        lhs = [q.astype(BF16), jnp.where((rw % 2) == 1, q * f, k).astype(BF16)]
        for L in HG_LEVELS[1:]:
            if 2 * L >= 8:
                ref = _bcast_row(b, 2 * L, L - 1)
            else:
                ref = jnp.where((rw % 8) < 4, _bcast_row(b, 8, 1), _bcast_row(b, 8, 5))
            el = jnp.exp2(-jnp.abs(b - ref))
            upper = (rw % (2 * L)) >= L
            lhs.append((jnp.where(upper, q, k) * el).astype(BF16))
        rhs = [k.astype(BF16)] + lhs[1:]
        masks = [ti == si] + [((ti // (2 * L)) == (si // (2 * L))) & ((ti % (2 * L)) >= L) & ((si % (2 * L)) < L)
                              for L in HG_LEVELS]
        for i, (hh, c) in enumerate(blocks):
            s_ = jnp.zeros((HG_CHUNK, HG_CHUNK), F32)
            for a_, b_, m_ in zip(lhs, rhs, masks):
                s_ = jnp.where(m_, _dot_nt(a_[row_(c), col(hh)], b_[row_(c), col(hh)]), s_)
            sc_scr[i] = s_.astype(BF16)

    for hh in range(HG_HEADS):
        st = st_ref[hh]
        for c in range(n_chunks):
            o_scr[row_(c), col(hh)] = (_dot(sc_scr[hh * n_chunks + c], vb16[row_(c), col(hh)])
                                       + _dot_nt(q_st[row_(c), col(hh)], st.astype(BF16)))
            last = (c + 1) * HG_CHUNK - 1
            st = st * dec_end[last:last + 1, col(hh)] + st_inc[hh, c]
        st_ref[hh] = st

    o = o_scr[...]
    hgn = hgn_ref[...]
    for hh in range(HG_HEADS):
        cs = col(hh)
        oh = o[:, cs]
        ms = jnp.mean(oh * oh, axis=-1, keepdims=True)
        mix_scr[:, cs] = (oh * lax.rsqrt(ms + LN_EPS) * hgn * gate_act[:, cs]).astype(BF16)

    mix = _dot(mix_scr[...], w_out_ref[...])
    o_ref[...] = _layer_norm(alpha * h + mix, g_ref[...], b_ref[...])


def _xattn_ffn_kernel(alpha, h_ref, k_ref, v_ref, wq_ref, wo_ref, g3_ref, b3_ref,
                      wg_ref, wu_ref, wd_ref, g4_ref, b4_ref, o_ref, a_scr, h_scr):
    h = h_ref[...]
    d_model = h.shape[1]
    hd = d_model // X_HEADS
    qb = (_dot(h.astype(BF16), wq_ref[...]) * (hd ** -0.5)).astype(BF16)
    for hh in range(X_HEADS):
        cs = slice(hh * hd, (hh + 1) * hd)
        s = _dot_nt(qb[:, cs], k_ref[:, cs])
        p = jnp.exp(s - jnp.max(s, axis=-1, keepdims=True))
        l = jnp.sum(p, axis=-1, keepdims=True)
        a_scr[:, cs] = (_dot(p.astype(BF16), v_ref[:, cs]) / l).astype(BF16)
    xa = _dot(a_scr[...], wo_ref[...])
    h3 = _layer_norm(alpha * h + xa, g3_ref[...], b3_ref[...])
    o_ref[...] = _swiglu_ln(h3, alpha, wg_ref, wu_ref, wd_ref, g4_ref, b4_ref, h_scr)


def _resident(shape):
    return pl.BlockSpec(shape, lambda *_: (0,) * len(shape), pipeline_mode=pl.Buffered(1))


def _params(n_grid_axes):
    return pltpu.CompilerParams(dimension_semantics=("arbitrary",) * n_grid_axes,
                                vmem_limit_bytes=VMEM_LIMIT_BYTES)


def kernel(x, mem, ffn1_w_gate, ffn1_w_up, ffn1_w_down, ln1_g, ln1_b, w_in, hg_lb_logits, hg_norm_g, sg_ln_g, sg_ln_b, sg_w_s, sg_b_s, w_out, ln2_g, ln2_b, mem_ln_g, mem_ln_b, xa_w_q, xa_w_k, xa_w_v, xa_w_o, ln3_g, ln3_b, ffn2_w_gate, ffn2_w_up, ffn2_w_down, ln4_g, ln4_b):
    depth = w_in.shape[0]
    assert depth == 1, "single-layer stack"
    B, T, D = x.shape
    n_rows = B * T
    d_ff = ffn1_w_gate.shape[2]
    alpha = (2.0 * depth) ** 0.25
    assert T % MIX_ROWS == 0 and T % FFN_ROWS == 0 and d_ff % FFN_COLS == 0
    assert MIX_ROWS % SG_CHUNK == 0 and MIX_ROWS % HG_CHUNK == 0

    row = lambda a: a.reshape(1, -1)
    bf = lambda a: a.astype(BF16)
    x2 = x.reshape(n_rows, D)
    mem2 = mem.reshape(B * MEM_LEN, D)

    k_mem, v_mem = pl.pallas_call(
        _kv_kernel,
        grid=(B,),
        in_specs=[pl.BlockSpec((MEM_LEN, D), lambda i: (i, 0)),
                  _resident((1, D)), _resident((1, D)), _resident((D, D)), _resident((D, D))],
        out_specs=[pl.BlockSpec((MEM_LEN, D), lambda i: (i, 0))] * 2,
        out_shape=[jax.ShapeDtypeStruct((B * MEM_LEN, D), BF16)] * 2,
        compiler_params=_params(1),
        name="memory_kv",
    )(mem2, row(mem_ln_g[0]), row(mem_ln_b[0]), bf(xa_w_k[0]), bf(xa_w_v[0]))

    row_spec = pl.BlockSpec((FFN_ROWS, D), lambda i: (i, 0))
    h1 = pl.pallas_call(
        functools.partial(_ffn_kernel, alpha),
        grid=(n_rows // FFN_ROWS,),
        in_specs=[row_spec, _resident((D, d_ff)), _resident((D, d_ff)), _resident((d_ff, D)),
                  _resident((1, D)), _resident((1, D))],
        out_specs=row_spec,
        out_shape=jax.ShapeDtypeStruct((n_rows, D), F32),
        scratch_shapes=[pltpu.VMEM((FFN_ROWS, d_ff), BF16)],
        compiler_params=_params(1),
        name="ffn1_ln1",
    )(x2, bf(ffn1_w_gate[0]), bf(ffn1_w_up[0]), bf(ffn1_w_down[0]), row(ln1_g[0]), row(ln1_b[0]))

    tiles_per_seq = T // MIX_ROWS
    in_width = w_in.shape[2]
    mix_spec = pl.BlockSpec((MIX_ROWS, D), lambda bi, ti: (bi * tiles_per_seq + ti, 0))
    h2 = pl.pallas_call(
        functools.partial(_mixer_kernel, alpha),
        grid=(B, tiles_per_seq),
        in_specs=[mix_spec, _resident((D, in_width)),
                  _resident((hg_lb_logits.shape[0], HG_WIDTH)), _resident((1, HG_DK)),
                  _resident((1, SG_WIDTH)), _resident((1, SG_WIDTH)),
                  _resident((SG_GROUPS, SG_CHUNK, SG_CHUNK)), _resident((SG_CHUNK, SG_GROUPS)),
                  _resident((D, D)), _resident((1, D)), _resident((1, D))],
        out_specs=mix_spec,
        out_shape=jax.ShapeDtypeStruct((n_rows, D), F32),
        scratch_shapes=[pltpu.VMEM((HG_HEADS, HG_DK, HG_DK), F32),
                        pltpu.VMEM((MIX_ROWS, HG_WIDTH), F32),
                        pltpu.VMEM((MIX_ROWS, D), BF16),
                        pltpu.VMEM((HG_HEADS * (MIX_ROWS // HG_CHUNK), HG_CHUNK, HG_CHUNK), BF16)],
        compiler_params=_params(2),
        name="mixer_ln2",
    )(h1, bf(w_in[0]), hg_lb_logits.reshape(hg_lb_logits.shape[0], HG_WIDTH), row(hg_norm_g[0]),
      row(sg_ln_g[0]), row(sg_ln_b[0]), sg_w_s[0], sg_b_s[0].T, bf(w_out[0]),
      row(ln2_g[0]), row(ln2_b[0]))

    tiles_per_batch = T // FFN_ROWS
    kv_spec = pl.BlockSpec((MEM_LEN, D), lambda i: (i // tiles_per_batch, 0))
    out = pl.pallas_call(
        functools.partial(_xattn_ffn_kernel, alpha),
        grid=(n_rows // FFN_ROWS,),
        in_specs=[row_spec, kv_spec, kv_spec, _resident((D, D)), _resident((D, D)),
                  _resident((1, D)), _resident((1, D)),
                  _resident((D, d_ff)), _resident((D, d_ff)), _resident((d_ff, D)),
                  _resident((1, D)), _resident((1, D))],
        out_specs=row_spec,
        out_shape=jax.ShapeDtypeStruct((n_rows, D), F32),
        scratch_shapes=[pltpu.VMEM((FFN_ROWS, D), BF16), pltpu.VMEM((FFN_ROWS, d_ff), BF16)],
        compiler_params=_params(1),
        name="xattn_ln3_ffn2_ln4",
    )(h2, k_mem, v_mem, bf(xa_w_q[0]), bf(xa_w_o[0]), row(ln3_g[0]), row(ln3_b[0]),
      bf(ffn2_w_gate[0]), bf(ffn2_w_up[0]), bf(ffn2_w_down[0]), row(ln4_g[0]), row(ln4_b[0]))

    return out.reshape(B, T, D)
```

```python
import functools

import jax
import jax.numpy as jnp
from jax import lax
from jax.experimental import pallas as pl
from jax.experimental.pallas import tpu as pltpu

F32 = jnp.float32
BF16 = jnp.bfloat16

LN_EPS = 1e-5
MEM_LEN = 256
HG_HEADS = 4
HG_DK = 128
HG_WIDTH = HG_HEADS * HG_DK
SG_GROUPS = 4
SG_DIM = 128
SG_WIDTH = SG_GROUPS * SG_DIM
SG_CHUNK = 128
X_HEADS = 4

HG_CHUNK = 64
HG_LEVELS = (1, 2, 4, 8, 16, 32)
HG_SAFE_LOG2 = 100.0

FFN1_ROWS = 1024
FFN_ROWS = 512
FFN_COLS = 256
MIX_ROWS = 512
CUMSUM_ROWS = 256
VMEM_LIMIT_BYTES = 56 * 1024 * 1024


def _dot(a, b):
    return jnp.dot(a, b, preferred_element_type=F32)


def _dot_nt(a, b):
    return lax.dot_general(a, b, (((1,), (1,)), ((), ())), preferred_element_type=F32)


def _dot_tn(a, b):
    return lax.dot_general(a, b, (((0,), (0,)), ((), ())), preferred_element_type=F32)


def _layer_norm(y, g, b):
    mu = jnp.mean(y, axis=-1, keepdims=True)
    d = y - mu
    var = jnp.mean(d * d, axis=-1, keepdims=True)
    return d * lax.rsqrt(var + LN_EPS) * g + b


def _sigmoid(x):
    return 1.0 / (1.0 + jnp.exp(-x))


def _swiglu_ln(x, alpha, wg_ref, wu_ref, wd_ref, g_ref, b_ref, h_scr):
    xb = x.astype(BF16)
    d_ff = wg_ref.shape[1]
    for j in range(d_ff // FFN_COLS):
        sl = slice(j * FFN_COLS, (j + 1) * FFN_COLS)
        gate = _dot(xb, wg_ref[:, sl])
        up = _dot(xb, wu_ref[:, sl])
        h_scr[:, sl] = (gate * _sigmoid(gate) * up).astype(BF16)
    y = _dot(h_scr[...], wd_ref[...])
    return _layer_norm(alpha * x + 0.5 * y, g_ref[...], b_ref[...])


def _kv_kernel(mem_ref, g_ref, b_ref, wk_ref, wv_ref, k_ref, v_ref):
    m = _layer_norm(mem_ref[...], g_ref[...], b_ref[...]).astype(BF16)
    k_ref[...] = _dot(m, wk_ref[...]).astype(BF16)
    v_ref[...] = _dot(m, wv_ref[...]).astype(BF16)


def _ffn_kernel(alpha, x_ref, wg_ref, wu_ref, wd_ref, g_ref, b_ref, o_ref, h_scr):
    o_ref[...] = _swiglu_ln(x_ref[...], alpha, wg_ref, wu_ref, wd_ref, g_ref, b_ref, h_scr)


def _bcast_row(a, blk, r):
    rows, w = a.shape
    a3 = a.reshape(rows // blk, blk, w)
    return jnp.broadcast_to(a3[:, r:r + 1, :], a3.shape).reshape(rows, w)


def _mixer_kernel(alpha, h_ref, w_in_ref, lbl_ref, hgn_ref, sglg_ref, sglb_ref, ws_ref,
                  bst_ref, w_out_ref, g_ref, b_ref, o_ref, st_ref, o_scr, mix_scr, sc_scr):
    rows = h_ref.shape[0]
    W = HG_WIDTH

    @pl.when(pl.program_id(1) == 0)
    def _():
        st_ref[...] = jnp.zeros_like(st_ref)

    h = h_ref[...]
    proj = _dot(h.astype(BF16), w_in_ref[...])
    q = proj[:, 0:W]
    fz = proj[:, W:2 * W]
    iv = proj[:, 2 * W:3 * W]
    gt = proj[:, 3 * W:4 * W]
    uv = proj[:, 4 * W:]

    lbl = lbl_ref[...]
    lbe = jnp.exp(lbl - jnp.max(lbl, axis=0, keepdims=True))
    lb = lbe[0:1, :] / jnp.sum(lbe, axis=0, keepdims=True)

    e = jnp.exp(-jnp.abs(fz))
    r = 1.0 / (1.0 + e)
    er = e * r
    pos = fz >= 0.0
    f = lb + (1.0 - lb) * jnp.where(pos, r, er)
    k = (1.0 - lb) * jnp.where(pos, er, r)
    lf = jnp.log2(f)

    ri = lax.broadcasted_iota(jnp.int32, (CUMSUM_ROWS, CUMSUM_ROWS), 0)
    ci = lax.broadcasted_iota(jnp.int32, (CUMSUM_ROWS, CUMSUM_ROWS), 1)
    tri = ((ri >= ci) & (ri // HG_CHUNK == ci // HG_CHUNK)).astype(BF16)
    lf0 = lf.astype(BF16)
    rem = lf - lf0.astype(F32)
    lf1 = rem.astype(BF16)
    lf2 = (rem - lf1.astype(F32)).astype(BF16)
    b = jnp.concatenate(
        [_dot(tri, lf0[rs, :]) + _dot(tri, lf1[rs, :]) + _dot(tri, lf2[rs, :])
         for rs in (slice(i, i + CUMSUM_ROWS) for i in range(0, rows, CUMSUM_ROWS))], axis=0)

    rw = lax.broadcasted_iota(jnp.int32, (rows, W), 0)
    vb16 = iv.astype(BF16)
    n_chunks = rows // HG_CHUNK
    blocks = [(hh, c) for hh in range(HG_HEADS) for c in range(n_chunks)]
    col = lambda hh: slice(hh * HG_DK, (hh + 1) * HG_DK)
    row_ = lambda c: slice(c * HG_CHUNK, (c + 1) * HG_CHUNK)
    ti = lax.broadcasted_iota(jnp.int32, (HG_CHUNK, HG_CHUNK), 0)
    si = lax.broadcasted_iota(jnp.int32, (HG_CHUNK, HG_CHUNK), 1)

    b_end = _bcast_row(b, HG_CHUNK, HG_CHUNK - 1)
    q_st = (q * jnp.exp2(b)).astype(BF16)
    k_st = (k * jnp.exp2(b_end - b)).astype(BF16)
    dec_end = jnp.exp2(b_end)
    st_inc = {(hh, c): _dot_tn(vb16[row_(c), col(hh)], k_st[row_(c), col(hh)]) for hh, c in blocks}
    gate_act = gt * _sigmoid(gt)

    uvg = uv * (0.5 * (1.0 + jnp.tanh(0.7978845608028654 * (uv + 0.044715 * (uv * uv * uv)))))
    causal = (lax.broadcasted_iota(jnp.int32, (SG_CHUNK, SG_CHUNK), 0)
              >= lax.broadcasted_iota(jnp.int32, (SG_CHUNK, SG_CHUNK), 1))
    for g in range(SG_GROUPS):
        cs = slice(g * SG_DIM, (g + 1) * SG_DIM)
        u_g = uvg[:, cs]
        v_g = uvg[:, SG_WIDTH + g * SG_DIM:SG_WIDTH + (g + 1) * SG_DIM]
        vn = _layer_norm(v_g, sglg_ref[:, cs], sglb_ref[:, cs]).astype(BF16)
        w_g = jnp.where(causal, ws_ref[g], 0.0).astype(BF16)
        bias = bst_ref[:, g:g + 1]
        for n in range(rows // SG_CHUNK):
            rs = slice(n * SG_CHUNK, (n + 1) * SG_CHUNK)
            s_ = _dot(w_g, vn[rs, :]) + bias
            mix_scr[rs, HG_WIDTH + g * SG_DIM:HG_WIDTH + (g + 1) * SG_DIM] = (u_g[rs, :] * s_).astype(BF16)

    safe =jnp.max(-b_end) < HG_SAFE_LOG2

    @pl.when(safe)
    def _():
        x = b - _bcast_row(b, HG_CHUNK, HG_CHUNK // 2 - 1)
        qm = (q * jnp.exp2(x)).astype(BF16)
        km = (k * jnp.exp2(-x)).astype(BF16)
        for i, (hh, c) in enumerate(blocks):
            s_ = _dot_nt(qm[row_(c), col(hh)], km[row_(c), col(hh)])
            sc_scr[i] = jnp.where(ti >= si, s_, 0.0).astype(BF16)

    @pl.when(jnp.logical_not(safe))
    def _():
        lhs = [q.astype(BF16), jnp.where((rw % 2) == 1, q * f, k).astype(BF16)]
        for L in HG_LEVELS[1:]:
            if 2 * L >= 8:
                ref = _bcast_row(b, 2 * L, L - 1)
            else:
                ref = jnp.where((rw % 8) < 4, _bcast_row(b, 8, 1), _bcast_row(b, 8, 5))
            el = jnp.exp2(-jnp.abs(b - ref))
            upper = (rw % (2 * L)) >= L
            lhs.append((jnp.where(upper, q, k) * el).astype(BF16))
        rhs = [k.astype(BF16)] + lhs[1:]
        masks = [ti == si] + [((ti // (2 * L)) == (si // (2 * L))) & ((ti % (2 * L)) >= L) & ((si % (2 * L)) < L)
                              for L in HG_LEVELS]
        for i, (hh, c) in enumerate(blocks):
            s_ = jnp.zeros((HG_CHUNK, HG_CHUNK), F32)
            for a_, b_, m_ in zip(lhs, rhs, masks):
                s_ = jnp.where(m_, _dot_nt(a_[row_(c), col(hh)], b_[row_(c), col(hh)]), s_)
            sc_scr[i] = s_.astype(BF16)

    for hh in range(HG_HEADS):
        st = st_ref[hh]
        for c in range(n_chunks):
            o_scr[row_(c), col(hh)] = (_dot(sc_scr[hh * n_chunks + c], vb16[row_(c), col(hh)])
                                       + _dot_nt(q_st[row_(c), col(hh)], st.astype(BF16)))
            last = (c + 1) * HG_CHUNK - 1
            st = st * dec_end[last:last + 1, col(hh)] + st_inc[hh, c]
        st_ref[hh] = st

    o = o_scr[...]
    hgn = hgn_ref[...]
    for hh in range(HG_HEADS):
        cs = col(hh)
        oh = o[:, cs]
        ms = jnp.mean(oh * oh, axis=-1, keepdims=True)
        mix_scr[:, cs] = (oh * lax.rsqrt(ms + LN_EPS) * hgn * gate_act[:, cs]).astype(BF16)

    mix = _dot(mix_scr[...], w_out_ref[...])
    o_ref[...] = _layer_norm(alpha * h + mix, g_ref[...], b_ref[...])


def _xattn_ffn_kernel(alpha, h_ref, k_ref, v_ref, wq_ref, wo_ref, g3_ref, b3_ref,
                      wg_ref, wu_ref, wd_ref, g4_ref, b4_ref, o_ref, a_scr, h_scr):
    h = h_ref[...]
    d_model = h.shape[1]
    hd = d_model // X_HEADS
    qb = (_dot(h.astype(BF16), wq_ref[...]) * (hd ** -0.5)).astype(BF16)
    for hh in range(X_HEADS):
        cs = slice(hh * hd, (hh + 1) * hd)
        s = _dot_nt(qb[:, cs], k_ref[:, cs])
        p = jnp.exp(s - jnp.max(s, axis=-1, keepdims=True))
        l = jnp.sum(p, axis=-1, keepdims=True)
        a_scr[:, cs] = (_dot(p.astype(BF16), v_ref[:, cs]) / l).astype(BF16)
    xa = _dot(a_scr[...], wo_ref[...])
    h3 = _layer_norm(alpha * h + xa, g3_ref[...], b3_ref[...])
    o_ref[...] = _swiglu_ln(h3, alpha, wg_ref, wu_ref, wd_ref, g4_ref, b4_ref, h_scr)


def _resident(shape):
    return pl.BlockSpec(shape, lambda *_: (0,) * len(shape), pipeline_mode=pl.Buffered(1))


def _params(n_grid_axes):
    return pltpu.CompilerParams(dimension_semantics=("arbitrary",) * n_grid_axes,
                                vmem_limit_bytes=VMEM_LIMIT_BYTES)


def kernel(x, mem, ffn1_w_gate, ffn1_w_up, ffn1_w_down, ln1_g, ln1_b, w_in, hg_lb_logits, hg_norm_g, sg_ln_g, sg_ln_b, sg_w_s, sg_b_s, w_out, ln2_g, ln2_b, mem_ln_g, mem_ln_b, xa_w_q, xa_w_k, xa_w_v, xa_w_o, ln3_g, ln3_b, ffn2_w_gate, ffn2_w_up, ffn2_w_down, ln4_g, ln4_b):
    depth = w_in.shape[0]
    assert depth == 1, "single-layer stack"
    B, T, D = x.shape
    n_rows = B * T
    d_ff = ffn1_w_gate.shape[2]
    alpha = (2.0 * depth) ** 0.25
    assert T % MIX_ROWS == 0 and T % FFN_ROWS == 0 and T % FFN1_ROWS == 0 and d_ff % FFN_COLS == 0
    assert MIX_ROWS % SG_CHUNK == 0 and MIX_ROWS % HG_CHUNK == 0

    row = lambda a: a.reshape(1, -1)
    bf = lambda a: a.astype(BF16)
    x2 = x.reshape(n_rows, D)
    mem2 = mem.reshape(B * MEM_LEN, D)

    k_mem, v_mem = pl.pallas_call(
        _kv_kernel,
        grid=(B,),
        in_specs=[pl.BlockSpec((MEM_LEN, D), lambda i: (i, 0)),
                  _resident((1, D)), _resident((1, D)), _resident((D, D)), _resident((D, D))],
        out_specs=[pl.BlockSpec((MEM_LEN, D), lambda i: (i, 0))] * 2,
        out_shape=[jax.ShapeDtypeStruct((B * MEM_LEN, D), BF16)] * 2,
        compiler_params=_params(1),
        name="memory_kv",
    )(mem2, row(mem_ln_g[0]), row(mem_ln_b[0]), bf(xa_w_k[0]), bf(xa_w_v[0]))

    row1_spec = pl.BlockSpec((FFN1_ROWS, D), lambda i: (i, 0))
    h1 = pl.pallas_call(
        functools.partial(_ffn_kernel, alpha),
        grid=(n_rows // FFN1_ROWS,),
        in_specs=[row1_spec, _resident((D, d_ff)), _resident((D, d_ff)), _resident((d_ff, D)),
                  _resident((1, D)), _resident((1, D))],
        out_specs=row1_spec,
        out_shape=jax.ShapeDtypeStruct((n_rows, D), F32),
        scratch_shapes=[pltpu.VMEM((FFN1_ROWS, d_ff), BF16)],
        compiler_params=_params(1),
        name="ffn1_ln1",
    )(x2, bf(ffn1_w_gate[0]), bf(ffn1_w_up[0]), bf(ffn1_w_down[0]), row(ln1_g[0]), row(ln1_b[0]))

    tiles_per_seq = T // MIX_ROWS
    in_width = w_in.shape[2]
    mix_spec = pl.BlockSpec((MIX_ROWS, D), lambda bi, ti: (bi * tiles_per_seq + ti, 0))
    h2 = pl.pallas_call(
        functools.partial(_mixer_kernel, alpha),
        grid=(B, tiles_per_seq),
        in_specs=[mix_spec, _resident((D, in_width)),
                  _resident((hg_lb_logits.shape[0], HG_WIDTH)), _resident((1, HG_DK)),
                  _resident((1, SG_WIDTH)), _resident((1, SG_WIDTH)),
                  _resident((SG_GROUPS, SG_CHUNK, SG_CHUNK)), _resident((SG_CHUNK, SG_GROUPS)),
                  _resident((D, D)), _resident((1, D)), _resident((1, D))],
        out_specs=mix_spec,
        out_shape=jax.ShapeDtypeStruct((n_rows, D), F32),
        scratch_shapes=[pltpu.VMEM((HG_HEADS, HG_DK, HG_DK), F32),
                        pltpu.VMEM((MIX_ROWS, HG_WIDTH), F32),
                        pltpu.VMEM((MIX_ROWS, D), BF16),
                        pltpu.VMEM((HG_HEADS * (MIX_ROWS // HG_CHUNK), HG_CHUNK, HG_CHUNK), BF16)],
        compiler_params=_params(2),
        name="mixer_ln2",
    )(h1, bf(w_in[0]), hg_lb_logits.reshape(hg_lb_logits.shape[0], HG_WIDTH), row(hg_norm_g[0]),
      row(sg_ln_g[0]), row(sg_ln_b[0]), sg_w_s[0], sg_b_s[0].T, bf(w_out[0]),
      row(ln2_g[0]), row(ln2_b[0]))

    row_spec = pl.BlockSpec((FFN_ROWS, D), lambda i: (i, 0))
    tiles_per_batch = T // FFN_ROWS
    kv_spec = pl.BlockSpec((MEM_LEN, D), lambda i: (i // tiles_per_batch, 0))
    out = pl.pallas_call(
        functools.partial(_xattn_ffn_kernel, alpha),
        grid=(n_rows // FFN_ROWS,),
        in_specs=[row_spec, kv_spec, kv_spec, _resident((D, D)), _resident((D, D)),
                  _resident((1, D)), _resident((1, D)),
                  _resident((D, d_ff)), _resident((D, d_ff)), _resident((d_ff, D)),
                  _resident((1, D)), _resident((1, D))],
        out_specs=row_spec,
        out_shape=jax.ShapeDtypeStruct((n_rows, D), F32),
        scratch_shapes=[pltpu.VMEM((FFN_ROWS, D), BF16), pltpu.VMEM((FFN_ROWS, d_ff), BF16)],
        compiler_params=_params(1),
        name="xattn_ln3_ffn2_ln4",
    )(h2, k_mem, v_mem, bf(xa_w_q[0]), bf(xa_w_o[0]), row(ln3_g[0]), row(ln3_b[0]),
      bf(ffn2_w_gate[0]), bf(ffn2_w_up[0]), bf(ffn2_w_down[0]), row(ln4_g[0]), row(ln4_b[0]))

    return out.reshape(B, T, D)
```

```python
import functools

import jax
import jax.numpy as jnp
from jax import lax
from jax.experimental import pallas as pl
from jax.experimental.pallas import tpu as pltpu

F32 = jnp.float32
BF16 = jnp.bfloat16

LN_EPS = 1e-5
MEM_LEN = 256
HG_HEADS = 4
HG_DK = 128
HG_WIDTH = HG_HEADS * HG_DK
SG_GROUPS = 4
SG_DIM = 128
SG_WIDTH = SG_GROUPS * SG_DIM
SG_CHUNK = 128
X_HEADS = 4

HG_CHUNK = 64
HG_LEVELS = (1, 2, 4, 8, 16, 32)
HG_SAFE_LOG2 = 100.0

FFN1_ROWS = 1024
FFN_ROWS = 1024
SUB_ROWS = 256
FFN_COLS = 256
MIX_ROWS = 512
CUMSUM_ROWS = 256
VMEM_LIMIT_BYTES = 56 * 1024 * 1024


def _dot(a, b):
    return jnp.dot(a, b, preferred_element_type=F32)


def _dot_nt(a, b):
    return lax.dot_general(a, b, (((1,), (1,)), ((), ())), preferred_element_type=F32)


def _dot_tn(a, b):
    return lax.dot_general(a, b, (((0,), (0,)), ((), ())), preferred_element_type=F32)


def _layer_norm(y, g, b):
    mu = jnp.mean(y, axis=-1, keepdims=True)
    d = y - mu
    var = jnp.mean(d * d, axis=-1, keepdims=True)
    return d * lax.rsqrt(var + LN_EPS) * g + b


def _sigmoid(x):
    return 1.0 / (1.0 + jnp.exp(-x))


def _swiglu_ln(x, alpha, wg_ref, wu_ref, wd_ref, g_ref, b_ref, h_scr):
    xb = x.astype(BF16)
    d_ff = wg_ref.shape[1]
    for j in range(d_ff // FFN_COLS):
        sl = slice(j * FFN_COLS, (j + 1) * FFN_COLS)
        gate = _dot(xb, wg_ref[:, sl])
        up = _dot(xb, wu_ref[:, sl])
        h_scr[:, sl] = (gate * _sigmoid(gate) * up).astype(BF16)
    y = _dot(h_scr[...], wd_ref[...])
    return _layer_norm(alpha * x + 0.5 * y, g_ref[...], b_ref[...])


def _kv_kernel(mem_ref, g_ref, b_ref, wk_ref, wv_ref, kt_ref, v_ref):
    m = _layer_norm(mem_ref[...], g_ref[...], b_ref[...]).astype(BF16)
    kt_ref[...] = _dot(m, wk_ref[...]).T.astype(BF16)
    v_ref[...] = _dot(m, wv_ref[...]).astype(BF16)


def _ffn_kernel(alpha, x_ref, wg_ref, wu_ref, wd_ref, g_ref, b_ref, o_ref, h_scr):
    for r in range(0, x_ref.shape[0], SUB_ROWS):
        rs = slice(r, r + SUB_ROWS)
        o_ref[rs, :] = _swiglu_ln(x_ref[rs, :], alpha, wg_ref, wu_ref, wd_ref, g_ref, b_ref, h_scr.at[rs, :])


def _bcast_row(a, blk, r):
    rows, w = a.shape
    a3 = a.reshape(rows // blk, blk, w)
    return jnp.broadcast_to(a3[:, r:r + 1, :], a3.shape).reshape(rows, w)


def _mixer_kernel(alpha, h_ref, w_in_ref, lbl_ref, hgn_ref, sglg_ref, sglb_ref, ws_ref,
                  bst_ref, w_out_ref, g_ref, b_ref, o_ref, st_ref, o_scr, mix_scr, sc_scr):
    rows = h_ref.shape[0]
    W = HG_WIDTH

    @pl.when(pl.program_id(1) == 0)
    def _():
        st_ref[...] = jnp.zeros_like(st_ref)

    h = h_ref[...]
    proj = _dot(h.astype(BF16), w_in_ref[...])
    q = proj[:, 0:W]
    fz = proj[:, W:2 * W]
    iv = proj[:, 2 * W:3 * W]
    gt = proj[:, 3 * W:4 * W]
    uv = proj[:, 4 * W:]

    lbl = lbl_ref[...]
    lbe = jnp.exp(lbl - jnp.max(lbl, axis=0, keepdims=True))
    lb = lbe[0:1, :] / jnp.sum(lbe, axis=0, keepdims=True)

    e = jnp.exp(-jnp.abs(fz))
    r = 1.0 / (1.0 + e)
    er = e * r
    pos = fz >= 0.0
    f = lb + (1.0 - lb) * jnp.where(pos, r, er)
    k = (1.0 - lb) * jnp.where(pos, er, r)
    lf = jnp.log2(f)

    ri = lax.broadcasted_iota(jnp.int32, (CUMSUM_ROWS, CUMSUM_ROWS), 0)
    ci = lax.broadcasted_iota(jnp.int32, (CUMSUM_ROWS, CUMSUM_ROWS), 1)
    tri = ((ri >= ci) & (ri // HG_CHUNK == ci // HG_CHUNK)).astype(BF16)
    lf0 = lf.astype(BF16)
    rem = lf - lf0.astype(F32)
    lf1 = rem.astype(BF16)
    lf2 = (rem - lf1.astype(F32)).astype(BF16)
    b = jnp.concatenate(
        [_dot(tri, lf0[rs, :]) + _dot(tri, lf1[rs, :]) + _dot(tri, lf2[rs, :])
         for rs in (slice(i, i + CUMSUM_ROWS) for i in range(0, rows, CUMSUM_ROWS))], axis=0)

    rw = lax.broadcasted_iota(jnp.int32, (rows, W), 0)
    vb16 = iv.astype(BF16)
    n_chunks = rows // HG_CHUNK
    blocks = [(hh, c) for hh in range(HG_HEADS) for c in range(n_chunks)]
    col = lambda hh: slice(hh * HG_DK, (hh + 1) * HG_DK)
    row_ = lambda c: slice(c * HG_CHUNK, (c + 1) * HG_CHUNK)
    ti = lax.broadcasted_iota(jnp.int32, (HG_CHUNK, HG_CHUNK), 0)
    si = lax.broadcasted_iota(jnp.int32, (HG_CHUNK, HG_CHUNK), 1)

    b_end = _bcast_row(b, HG_CHUNK, HG_CHUNK - 1)
    q_st = (q * jnp.exp2(b)).astype(BF16)
    k_st = (k * jnp.exp2(b_end - b)).astype(BF16)
    dec_end = jnp.exp2(b_end)
    st_inc = {(hh, c): _dot_tn(vb16[row_(c), col(hh)], k_st[row_(c), col(hh)]) for hh, c in blocks}
    gate_act = gt * _sigmoid(gt)

    uvg = uv * (0.5 * (1.0 + jnp.tanh(0.7978845608028654 * (uv + 0.044715 * (uv * uv * uv)))))
    causal = (lax.broadcasted_iota(jnp.int32, (SG_CHUNK, SG_CHUNK), 0)
              >= lax.broadcasted_iota(jnp.int32, (SG_CHUNK, SG_CHUNK), 1))
    for g in range(SG_GROUPS):
        cs = slice(g * SG_DIM, (g + 1) * SG_DIM)
        u_g = uvg[:, cs]
        v_g = uvg[:, SG_WIDTH + g * SG_DIM:SG_WIDTH + (g + 1) * SG_DIM]
        vn = _layer_norm(v_g, sglg_ref[:, cs], sglb_ref[:, cs]).astype(BF16)
        w_g = jnp.where(causal, ws_ref[g], 0.0).astype(BF16)
        bias = bst_ref[:, g:g + 1]
        for n in range(rows // SG_CHUNK):
            rs = slice(n * SG_CHUNK, (n + 1) * SG_CHUNK)
            s_ = _dot(w_g, vn[rs, :]) + bias
            mix_scr[rs, HG_WIDTH + g * SG_DIM:HG_WIDTH + (g + 1) * SG_DIM] = (u_g[rs, :] * s_).astype(BF16)

    safe =jnp.max(-b_end) < HG_SAFE_LOG2

    @pl.when(safe)
    def _():
        x = b - _bcast_row(b, HG_CHUNK, HG_CHUNK // 2 - 1)
        qm = (q * jnp.exp2(x)).astype(BF16)
        km = (k * jnp.exp2(-x)).astype(BF16)
        for i, (hh, c) in enumerate(blocks):
            s_ = _dot_nt(qm[row_(c), col(hh)], km[row_(c), col(hh)])
            sc_scr[i] = jnp.where(ti >= si, s_, 0.0).astype(BF16)

    @pl.when(jnp.logical_not(safe))
    def _():
        lhs = [q.astype(BF16), jnp.where((rw % 2) == 1, q * f, k).astype(BF16)]
        for L in HG_LEVELS[1:]:
            if 2 * L >= 8:
                ref = _bcast_row(b, 2 * L, L - 1)
            else:
                ref = jnp.where((rw % 8) < 4, _bcast_row(b, 8, 1), _bcast_row(b, 8, 5))
            el = jnp.exp2(-jnp.abs(b - ref))
            upper = (rw % (2 * L)) >= L
            lhs.append((jnp.where(upper, q, k) * el).astype(BF16))
        rhs = [k.astype(BF16)] + lhs[1:]
        masks = [ti == si] + [((ti // (2 * L)) == (si // (2 * L))) & ((ti % (2 * L)) >= L) & ((si % (2 * L)) < L)
                              for L in HG_LEVELS]
        for i, (hh, c) in enumerate(blocks):
            s_ = jnp.zeros((HG_CHUNK, HG_CHUNK), F32)
            for a_, b_, m_ in zip(lhs, rhs, masks):
                s_ = jnp.where(m_, _dot_nt(a_[row_(c), col(hh)], b_[row_(c), col(hh)]), s_)
            sc_scr[i] = s_.astype(BF16)

    for hh in range(HG_HEADS):
        st = st_ref[hh]
        for c in range(n_chunks):
            o_scr[row_(c), col(hh)] = (_dot(sc_scr[hh * n_chunks + c], vb16[row_(c), col(hh)])
                                       + _dot_nt(q_st[row_(c), col(hh)], st.astype(BF16)))
            last = (c + 1) * HG_CHUNK - 1
            st = st * dec_end[last:last + 1, col(hh)] + st_inc[hh, c]
        st_ref[hh] = st

    o = o_scr[...]
    hgn = hgn_ref[...]
    for hh in range(HG_HEADS):
        cs = col(hh)
        oh = o[:, cs]
        ms = jnp.mean(oh * oh, axis=-1, keepdims=True)
        mix_scr[:, cs] = (oh * lax.rsqrt(ms + LN_EPS) * hgn * gate_act[:, cs]).astype(BF16)

    mix = _dot(mix_scr[...], w_out_ref[...])
    o_ref[...] = _layer_norm(alpha * h + mix, g_ref[...], b_ref[...])


def _xattn_ffn_kernel(alpha, h_ref, kt_ref, v_ref, wq_ref, wo_ref, g3_ref, b3_ref,
                      wg_ref, wu_ref, wd_ref, g4_ref, b4_ref, o_ref, a_scr, h_scr):
    d_model = h_ref.shape[1]
    hd = d_model // X_HEADS
    for r in range(0, h_ref.shape[0], SUB_ROWS):
        rs = slice(r, r + SUB_ROWS)
        h = h_ref[rs, :]
        qb = (_dot(h.astype(BF16), wq_ref[...]) * (hd ** -0.5)).astype(BF16)
        for hh in range(X_HEADS):
            cs = slice(hh * hd, (hh + 1) * hd)
            s = _dot(qb[:, cs], kt_ref[cs, :])
            p = jnp.exp(s - jnp.max(s, axis=-1, keepdims=True))
            l = jnp.sum(p, axis=-1, keepdims=True)
            a_scr[rs, cs] = (_dot(p.astype(BF16), v_ref[:, cs]) / l).astype(BF16)
        xa = _dot(a_scr[rs, :], wo_ref[...])
        h3 = _layer_norm(alpha * h + xa, g3_ref[...], b3_ref[...])
        o_ref[rs, :] = _swiglu_ln(h3, alpha, wg_ref, wu_ref, wd_ref, g4_ref, b4_ref, h_scr.at[rs, :])


def _resident(shape):
    return pl.BlockSpec(shape, lambda *_: (0,) * len(shape), pipeline_mode=pl.Buffered(1))


def _params(n_grid_axes):
    return pltpu.CompilerParams(dimension_semantics=("arbitrary",) * n_grid_axes,
                                vmem_limit_bytes=VMEM_LIMIT_BYTES)


def kernel(x, mem, ffn1_w_gate, ffn1_w_up, ffn1_w_down, ln1_g, ln1_b, w_in, hg_lb_logits, hg_norm_g, sg_ln_g, sg_ln_b, sg_w_s, sg_b_s, w_out, ln2_g, ln2_b, mem_ln_g, mem_ln_b, xa_w_q, xa_w_k, xa_w_v, xa_w_o, ln3_g, ln3_b, ffn2_w_gate, ffn2_w_up, ffn2_w_down, ln4_g, ln4_b):
    depth = w_in.shape[0]
    assert depth == 1, "single-layer stack"
    B, T, D = x.shape
    n_rows = B * T
    d_ff = ffn1_w_gate.shape[2]
    alpha = (2.0 * depth) ** 0.25
    assert T % MIX_ROWS == 0 and T % FFN_ROWS == 0 and T % FFN1_ROWS == 0 and d_ff % FFN_COLS == 0
    assert MIX_ROWS % SG_CHUNK == 0 and MIX_ROWS % HG_CHUNK == 0

    row = lambda a: a.reshape(1, -1)
    bf = lambda a: a.astype(BF16)
    x2 = x.reshape(n_rows, D)
    mem2 = mem.reshape(B * MEM_LEN, D)

    kt_mem, v_mem = pl.pallas_call(
        _kv_kernel,
        grid=(B,),
        in_specs=[pl.BlockSpec((MEM_LEN, D), lambda i: (i, 0)),
                  _resident((1, D)), _resident((1, D)), _resident((D, D)), _resident((D, D))],
        out_specs=[pl.BlockSpec((D, MEM_LEN), lambda i: (i, 0)), pl.BlockSpec((MEM_LEN, D), lambda i: (i, 0))],
        out_shape=[jax.ShapeDtypeStruct((B * D, MEM_LEN), BF16), jax.ShapeDtypeStruct((B * MEM_LEN, D), BF16)],
        compiler_params=_params(1),
        name="memory_kv",
    )(mem2, row(mem_ln_g[0]), row(mem_ln_b[0]), bf(xa_w_k[0]), bf(xa_w_v[0]))

    row1_spec = pl.BlockSpec((FFN1_ROWS, D), lambda i: (i, 0))
    h1 = pl.pallas_call(
        functools.partial(_ffn_kernel, alpha),
        grid=(n_rows // FFN1_ROWS,),
        in_specs=[row1_spec, _resident((D, d_ff)), _resident((D, d_ff)), _resident((d_ff, D)),
                  _resident((1, D)), _resident((1, D))],
        out_specs=row1_spec,
        out_shape=jax.ShapeDtypeStruct((n_rows, D), F32),
        scratch_shapes=[pltpu.VMEM((FFN1_ROWS, d_ff), BF16)],
        compiler_params=_params(1),
        name="ffn1_ln1",
    )(x2, bf(ffn1_w_gate[0]), bf(ffn1_w_up[0]), bf(ffn1_w_down[0]), row(ln1_g[0]), row(ln1_b[0]))

    tiles_per_seq = T // MIX_ROWS
    in_width = w_in.shape[2]
    mix_spec = pl.BlockSpec((MIX_ROWS, D), lambda bi, ti: (bi * tiles_per_seq + ti, 0))
    h2 = pl.pallas_call(
        functools.partial(_mixer_kernel, alpha),
        grid=(B, tiles_per_seq),
        in_specs=[mix_spec, _resident((D, in_width)),
                  _resident((hg_lb_logits.shape[0], HG_WIDTH)), _resident((1, HG_DK)),
                  _resident((1, SG_WIDTH)), _resident((1, SG_WIDTH)),
                  _resident((SG_GROUPS, SG_CHUNK, SG_CHUNK)), _resident((SG_CHUNK, SG_GROUPS)),
                  _resident((D, D)), _resident((1, D)), _resident((1, D))],
        out_specs=mix_spec,
        out_shape=jax.ShapeDtypeStruct((n_rows, D), F32),
        scratch_shapes=[pltpu.VMEM((HG_HEADS, HG_DK, HG_DK), F32),
                        pltpu.VMEM((MIX_ROWS, HG_WIDTH), F32),
                        pltpu.VMEM((MIX_ROWS, D), BF16),
                        pltpu.VMEM((HG_HEADS * (MIX_ROWS // HG_CHUNK), HG_CHUNK, HG_CHUNK), BF16)],
        compiler_params=_params(2),
        name="mixer_ln2",
    )(h1, bf(w_in[0]), hg_lb_logits.reshape(hg_lb_logits.shape[0], HG_WIDTH), row(hg_norm_g[0]),
      row(sg_ln_g[0]), row(sg_ln_b[0]), sg_w_s[0], sg_b_s[0].T, bf(w_out[0]),
      row(ln2_g[0]), row(ln2_b[0]))

    row_spec = pl.BlockSpec((FFN_ROWS, D), lambda i: (i, 0))
    tiles_per_batch = T // FFN_ROWS
    kv_spec = pl.BlockSpec((MEM_LEN, D), lambda i: (i // tiles_per_batch, 0))
    kt_spec = pl.BlockSpec((D, MEM_LEN), lambda i: (i // tiles_per_batch, 0))
    out = pl.pallas_call(
        functools.partial(_xattn_ffn_kernel, alpha),
        grid=(n_rows // FFN_ROWS,),
        in_specs=[row_spec, kt_spec, kv_spec, _resident((D, D)), _resident((D, D)),
                  _resident((1, D)), _resident((1, D)),
                  _resident((D, d_ff)), _resident((D, d_ff)), _resident((d_ff, D)),
                  _resident((1, D)), _resident((1, D))],
        out_specs=row_spec,
        out_shape=jax.ShapeDtypeStruct((n_rows, D), F32),
        scratch_shapes=[pltpu.VMEM((FFN_ROWS, D), BF16), pltpu.VMEM((FFN_ROWS, d_ff), BF16)],
        compiler_params=_params(1),
        name="xattn_ln3_ffn2_ln4",
    )(h2, kt_mem, v_mem, bf(xa_w_q[0]), bf(xa_w_o[0]), row(ln3_g[0]), row(ln3_b[0]),
      bf(ffn2_w_gate[0]), bf(ffn2_w_up[0]), bf(ffn2_w_down[0]), row(ln4_g[0]), row(ln4_b[0]))

    return out.reshape(B, T, D)
```

```python
import functools

import jax
import jax.numpy as jnp
from jax import lax
from jax.experimental import pallas as pl
from jax.experimental.pallas import tpu as pltpu

F32 = jnp.float32
BF16 = jnp.bfloat16

LN_EPS = 1e-5
MEM_LEN = 256
HG_HEADS = 4
HG_DK = 128
HG_WIDTH = HG_HEADS * HG_DK
SG_GROUPS = 4
SG_DIM = 128
SG_WIDTH = SG_GROUPS * SG_DIM
SG_CHUNK = 128
X_HEADS = 4

HG_CHUNK = 64
HG_LEVELS = (1, 2, 4, 8, 16, 32)
HG_SAFE_LOG2 = 100.0

FFN1_ROWS = 1024
FFN_ROWS = 512
SUB_ROWS = 256
FFN_COLS = 256
MIX_ROWS = 512
CUMSUM_ROWS = 256
OUT_ROWS = 128
LOAD_ROWS = 256
VMEM_LIMIT_BYTES = 56 * 1024 * 1024


def _dot(a, b):
    return jnp.dot(a, b, preferred_element_type=F32)


def _dot_nt(a, b):
    return lax.dot_general(a, b, (((1,), (1,)), ((), ())), preferred_element_type=F32)


def _dot_tn(a, b):
    return lax.dot_general(a, b, (((0,), (0,)), ((), ())), preferred_element_type=F32)


def _layer_norm(y, g, b):
    mu = jnp.mean(y, axis=-1, keepdims=True)
    d = y - mu
    var = jnp.mean(d * d, axis=-1, keepdims=True)
    return d * lax.rsqrt(var + LN_EPS) * g + b


def _sigmoid(x):
    return 1.0 / (1.0 + jnp.exp(-x))


def _load_weights_bf16(pairs, stage, sem):
    chunks = [(src, dst, r) for src, dst in pairs for r in range(0, src.shape[0], LOAD_ROWS)]

    def chunk_copy(i):
        src, _, r = chunks[i]
        return pltpu.make_async_copy(src.at[pl.ds(r, LOAD_ROWS), :],
                                     stage.at[i % 2, :, pl.ds(0, src.shape[1])], sem.at[i % 2])

    chunk_copy(0).start()
    for i, (src, dst, r) in enumerate(chunks):
        if i + 1 < len(chunks):
            chunk_copy(i + 1).start()
        chunk_copy(i).wait()
        dst[pl.ds(r, LOAD_ROWS), :] = stage[i % 2, :, 0:src.shape[1]].astype(BF16)


def _swiglu_ln(x, alpha, wg_ref, wu_ref, wd_ref, g_ref, b_ref, h_scr):
    xb = x.astype(BF16)
    d_ff = wg_ref.shape[1]
    for j in range(d_ff // FFN_COLS):
        sl = slice(j * FFN_COLS, (j + 1) * FFN_COLS)
        gate = _dot(xb, wg_ref[:, sl])
        up = _dot(xb, wu_ref[:, sl])
        h_scr[:, sl] = (gate * _sigmoid(gate) * up).astype(BF16)
    y = _dot(h_scr[...], wd_ref[...])
    return _layer_norm(alpha * x + 0.5 * y, g_ref[...], b_ref[...])


def _kv_kernel(mem_ref, g_ref, b_ref, wk_hbm, wv_hbm, k_ref, v_ref, wk_ref, wv_ref, stage, sem):
    @pl.when(pl.program_id(0) == 0)
    def _():
        _load_weights_bf16([(wk_hbm, wk_ref), (wv_hbm, wv_ref)], stage, sem)

    m = _layer_norm(mem_ref[...], g_ref[...], b_ref[...]).astype(BF16)
    k_ref[...] = _dot(m, wk_ref[...]).astype(BF16)
    v_ref[...] = _dot(m, wv_ref[...]).astype(BF16)


def _ffn_kernel(alpha, x_ref, wg_hbm, wu_hbm, wd_hbm, g_ref, b_ref, o_ref, h_scr,
                wg_ref, wu_ref, wd_ref, stage, sem):
    @pl.when(pl.program_id(0) == 0)
    def _():
        _load_weights_bf16([(wg_hbm, wg_ref), (wu_hbm, wu_ref), (wd_hbm, wd_ref)], stage, sem)

    for r in range(0, x_ref.shape[0], SUB_ROWS):
        rs = slice(r, r + SUB_ROWS)
        o_ref[rs, :] = _swiglu_ln(x_ref[rs, :], alpha, wg_ref, wu_ref, wd_ref, g_ref, b_ref, h_scr.at[rs, :])


def _bcast_row(a, blk, r):
    rows, w = a.shape
    a3 = a.reshape(rows // blk, blk, w)
    return jnp.broadcast_to(a3[:, r:r + 1, :], a3.shape).reshape(rows, w)


def _mixer_kernel(alpha, h_ref, w_in_hbm, lbl_ref, hgn_ref, sglg_ref, sglb_ref, ws_ref,
                  bst_ref, w_out_hbm, g_ref, b_ref, o_ref, st_ref, o_scr, mix_scr, sc_scr,
                  w_in_ref, w_out_ref, stage, sem):
    rows = h_ref.shape[0]
    W = HG_WIDTH

    @pl.when((pl.program_id(0) == 0) & (pl.program_id(1) == 0))
    def _():
        _load_weights_bf16([(w_in_hbm, w_in_ref), (w_out_hbm, w_out_ref)], stage, sem)

    @pl.when(pl.program_id(1) == 0)
    def _():
        st_ref[...] = jnp.zeros_like(st_ref)

    h = h_ref[...]
    proj = _dot(h.astype(BF16), w_in_ref[...])
    q = proj[:, 0:W]
    fz = proj[:, W:2 * W]
    iv = proj[:, 2 * W:3 * W]
    gt = proj[:, 3 * W:4 * W]
    uv = proj[:, 4 * W:]

    lbl = lbl_ref[...]
    lbe = jnp.exp(lbl - jnp.max(lbl, axis=0, keepdims=True))
    lb = lbe[0:1, :] / jnp.sum(lbe, axis=0, keepdims=True)

    e = jnp.exp(-jnp.abs(fz))
    r = 1.0 / (1.0 + e)
    er = e * r
    pos = fz >= 0.0
    f = lb + (1.0 - lb) * jnp.where(pos, r, er)
    k = (1.0 - lb) * jnp.where(pos, er, r)
    lf = jnp.log2(f)

    ri = lax.broadcasted_iota(jnp.int32, (CUMSUM_ROWS, CUMSUM_ROWS), 0)
    ci = lax.broadcasted_iota(jnp.int32, (CUMSUM_ROWS, CUMSUM_ROWS), 1)
    tri = ((ri >= ci) & (ri // HG_CHUNK == ci // HG_CHUNK)).astype(BF16)
    lf0 = lf.astype(BF16)
    rem = lf - lf0.astype(F32)
    lf1 = rem.astype(BF16)
    lf2 = (rem - lf1.astype(F32)).astype(BF16)
    b = jnp.concatenate(
        [_dot(tri, lf0[rs, :]) + _dot(tri, lf1[rs, :]) + _dot(tri, lf2[rs, :])
         for rs in (slice(i, i + CUMSUM_ROWS) for i in range(0, rows, CUMSUM_ROWS))], axis=0)

    rw = lax.broadcasted_iota(jnp.int32, (rows, W), 0)
    vb16 = iv.astype(BF16)
    n_chunks = rows // HG_CHUNK
    blocks = [(hh, c) for hh in range(HG_HEADS) for c in range(n_chunks)]
    col = lambda hh: slice(hh * HG_DK, (hh + 1) * HG_DK)
    row_ = lambda c: slice(c * HG_CHUNK, (c + 1) * HG_CHUNK)
    ti = lax.broadcasted_iota(jnp.int32, (HG_CHUNK, HG_CHUNK), 0)
    si = lax.broadcasted_iota(jnp.int32, (HG_CHUNK, HG_CHUNK), 1)

    b_end = _bcast_row(b, HG_CHUNK, HG_CHUNK - 1)
    q_st = (q * jnp.exp2(b)).astype(BF16)
    k_st = (k * jnp.exp2(b_end - b)).astype(BF16)
    dec_end = jnp.exp2(b_end)
    st_inc = {(hh, c): _dot_tn(vb16[row_(c), col(hh)], k_st[row_(c), col(hh)]) for hh, c in blocks}
    gate_act = gt * _sigmoid(gt)

    uvg = uv * (0.5 * (1.0 + jnp.tanh(0.7978845608028654 * (uv + 0.044715 * (uv * uv * uv)))))
    causal = (lax.broadcasted_iota(jnp.int32, (SG_CHUNK, SG_CHUNK), 0)
              >= lax.broadcasted_iota(jnp.int32, (SG_CHUNK, SG_CHUNK), 1))
    for g in range(SG_GROUPS):
        cs = slice(g * SG_DIM, (g + 1) * SG_DIM)
        u_g = uvg[:, cs]
        v_g = uvg[:, SG_WIDTH + g * SG_DIM:SG_WIDTH + (g + 1) * SG_DIM]
        vn = _layer_norm(v_g, sglg_ref[:, cs], sglb_ref[:, cs]).astype(BF16)
        w_g = jnp.where(causal, ws_ref[g], 0.0).astype(BF16)
        bias = bst_ref[:, g:g + 1]
        for n in range(rows // SG_CHUNK):
            rs = slice(n * SG_CHUNK, (n + 1) * SG_CHUNK)
            s_ = _dot(w_g, vn[rs, :]) + bias
            mix_scr[rs, HG_WIDTH + g * SG_DIM:HG_WIDTH + (g + 1) * SG_DIM] = (u_g[rs, :] * s_).astype(BF16)

    safe = jnp.max(-b_end) < HG_SAFE_LOG2

    @pl.when(safe)
    def _():
        x = b - _bcast_row(b, HG_CHUNK, HG_CHUNK // 2 - 1)
        qm = (q * jnp.exp2(x)).astype(BF16)
        km = (k * jnp.exp2(-x)).astype(BF16)
        for i, (hh, c) in enumerate(blocks):
            s_ = _dot_nt(qm[row_(c), col(hh)], km[row_(c), col(hh)])
            sc_scr[i] = jnp.where(ti >= si, s_, 0.0).astype(BF16)

    @pl.when(jnp.logical_not(safe))
    def _():
        lhs = [q.astype(BF16), jnp.where((rw % 2) == 1, q * f, k).astype(BF16)]
        for L in HG_LEVELS[1:]:
            if 2 * L >= 8:
                ref = _bcast_row(b, 2 * L, L - 1)
            else:
                ref = jnp.where((rw % 8) < 4, _bcast_row(b, 8, 1), _bcast_row(b, 8, 5))
            el = jnp.exp2(-jnp.abs(b - ref))
            upper = (rw % (2 * L)) >= L
            lhs.append((jnp.where(upper, q, k) * el).astype(BF16))
        rhs = [k.astype(BF16)] + lhs[1:]
        masks = [ti == si] + [((ti // (2 * L)) == (si // (2 * L))) & ((ti % (2 * L)) >= L) & ((si % (2 * L)) < L)
                              for L in HG_LEVELS]
        for i, (hh, c) in enumerate(blocks):
            s_ = jnp.zeros((HG_CHUNK, HG_CHUNK), F32)
            for a_, b_, m_ in zip(lhs, rhs, masks):
                s_ = jnp.where(m_, _dot_nt(a_[row_(c), col(hh)], b_[row_(c), col(hh)]), s_)
            sc_scr[i] = s_.astype(BF16)

    for hh in range(HG_HEADS):
        st = st_ref[hh]
        for c in range(n_chunks):
            o_scr[row_(c), col(hh)] = (_dot(sc_scr[hh * n_chunks + c], vb16[row_(c), col(hh)])
                                       + _dot_nt(q_st[row_(c), col(hh)], st.astype(BF16)))
            last = (c + 1) * HG_CHUNK - 1
            st = st * dec_end[last:last + 1, col(hh)] + st_inc[hh, c]
        st_ref[hh] = st

    o = o_scr[...]
    hgn = hgn_ref[...]
    for hh in range(HG_HEADS):
        cs = col(hh)
        oh = o[:, cs]
        ms = jnp.mean(oh * oh, axis=-1, keepdims=True)
        mix_scr[:, cs] = (oh * lax.rsqrt(ms + LN_EPS) * hgn * gate_act[:, cs]).astype(BF16)

    for r in range(0, rows, OUT_ROWS):
        rs = slice(r, r + OUT_ROWS)
        mix = _dot(mix_scr[rs, :], w_out_ref[...])
        o_ref[rs, :] = _layer_norm(alpha * h[rs, :] + mix, g_ref[...], b_ref[...])


def _xattn_ffn_kernel(alpha, h_ref, k_ref, v_ref, wq_hbm, wo_hbm, g3_ref, b3_ref,
                      wg_hbm, wu_hbm, wd_hbm, g4_ref, b4_ref, o_ref, a_scr, h_scr,
                      wq_ref, wo_ref, wg_ref, wu_ref, wd_ref, stage, sem):
    @pl.when(pl.program_id(0) == 0)
    def _():
        _load_weights_bf16([(wq_hbm, wq_ref), (wo_hbm, wo_ref), (wg_hbm, wg_ref), (wu_hbm, wu_ref),
                            (wd_hbm, wd_ref)], stage, sem)

    h = h_ref[...]
    d_model = h.shape[1]
    hd = d_model // X_HEADS
    qb = (_dot(h.astype(BF16), wq_ref[...]) * (hd ** -0.5)).astype(BF16)
    for hh in range(X_HEADS):
        cs = slice(hh * hd, (hh + 1) * hd)
        s = _dot_nt(qb[:, cs], k_ref[:, cs])
        p = jnp.exp(s - jnp.max(s, axis=-1, keepdims=True))
        l = jnp.sum(p, axis=-1, keepdims=True)
        a_scr[:, cs] = (_dot(p.astype(BF16), v_ref[:, cs]) / l).astype(BF16)
    xa = _dot(a_scr[...], wo_ref[...])
    h3 = _layer_norm(alpha * h + xa, g3_ref[...], b3_ref[...])
    o_ref[...] = _swiglu_ln(h3, alpha, wg_ref, wu_ref, wd_ref, g4_ref, b4_ref, h_scr)


def _resident(shape):
    return pl.BlockSpec(shape, lambda *_: (0,) * len(shape), pipeline_mode=pl.Buffered(1))


_HBM = pl.BlockSpec(memory_space=pl.ANY)


def _weight_scratch(shapes):
    width = max(c for _, c in shapes)
    return ([pltpu.VMEM(shape, BF16) for shape in shapes]
            + [pltpu.VMEM((2, LOAD_ROWS, width), F32), pltpu.SemaphoreType.DMA((2,))])


def _params(n_grid_axes):
    return pltpu.CompilerParams(dimension_semantics=("arbitrary",) * n_grid_axes,
                                vmem_limit_bytes=VMEM_LIMIT_BYTES)


def kernel(x, mem, ffn1_w_gate, ffn1_w_up, ffn1_w_down, ln1_g, ln1_b, w_in, hg_lb_logits, hg_norm_g, sg_ln_g, sg_ln_b, sg_w_s, sg_b_s, w_out, ln2_g, ln2_b, mem_ln_g, mem_ln_b, xa_w_q, xa_w_k, xa_w_v, xa_w_o, ln3_g, ln3_b, ffn2_w_gate, ffn2_w_up, ffn2_w_down, ln4_g, ln4_b):
    depth = w_in.shape[0]
    assert depth == 1, "single-layer stack"
    B, T, D = x.shape
    n_rows = B * T
    d_ff = ffn1_w_gate.shape[2]
    alpha = (2.0 * depth) ** 0.25
    assert T % MIX_ROWS == 0 and T % FFN_ROWS == 0 and T % FFN1_ROWS == 0 and d_ff % FFN_COLS == 0
    assert MIX_ROWS % SG_CHUNK == 0 and MIX_ROWS % HG_CHUNK == 0
    assert D % LOAD_ROWS == 0 and d_ff % LOAD_ROWS == 0

    row = lambda a: a.reshape(1, -1)
    mat = lambda a: a.reshape(a.shape[1:])
    x2 = x.reshape(n_rows, D)
    mem2 = mem.reshape(B * MEM_LEN, D)

    k_mem, v_mem = pl.pallas_call(
        _kv_kernel,
        grid=(B,),
        in_specs=[pl.BlockSpec((MEM_LEN, D), lambda i: (i, 0)),
                  _resident((1, D)), _resident((1, D)), _HBM, _HBM],
        out_specs=[pl.BlockSpec((MEM_LEN, D), lambda i: (i, 0))] * 2,
        out_shape=[jax.ShapeDtypeStruct((B * MEM_LEN, D), BF16)] * 2,
        scratch_shapes=_weight_scratch([(D, D), (D, D)]),
        compiler_params=_params(1),
        name="memory_kv",
    )(mem2, row(mem_ln_g[0]), row(mem_ln_b[0]), mat(xa_w_k), mat(xa_w_v))

    row1_spec = pl.BlockSpec((FFN1_ROWS, D), lambda i: (i, 0))
    h1 = pl.pallas_call(
        functools.partial(_ffn_kernel, alpha),
        grid=(n_rows // FFN1_ROWS,),
        in_specs=[row1_spec, _HBM, _HBM, _HBM, _resident((1, D)), _resident((1, D))],
        out_specs=row1_spec,
        out_shape=jax.ShapeDtypeStruct((n_rows, D), F32),
        scratch_shapes=[pltpu.VMEM((FFN1_ROWS, d_ff), BF16)]
        + _weight_scratch([(D, d_ff), (D, d_ff), (d_ff, D)]),
        compiler_params=_params(1),
        name="ffn1_ln1",
    )(x2, mat(ffn1_w_gate), mat(ffn1_w_up), mat(ffn1_w_down), row(ln1_g[0]), row(ln1_b[0]))

    tiles_per_seq = T // MIX_ROWS
    in_width = w_in.shape[2]
    mix_spec = pl.BlockSpec((MIX_ROWS, D), lambda bi, ti: (bi * tiles_per_seq + ti, 0))
    h2 = pl.pallas_call(
        functools.partial(_mixer_kernel, alpha),
        grid=(B, tiles_per_seq),
        in_specs=[mix_spec, _HBM,
                  _resident((hg_lb_logits.shape[0], HG_WIDTH)), _resident((1, HG_DK)),
                  _resident((1, SG_WIDTH)), _resident((1, SG_WIDTH)),
                  _resident((SG_GROUPS, SG_CHUNK, SG_CHUNK)), _resident((SG_CHUNK, SG_GROUPS)),
                  _HBM, _resident((1, D)), _resident((1, D))],
        out_specs=mix_spec,
        out_shape=jax.ShapeDtypeStruct((n_rows, D), F32),
        scratch_shapes=[pltpu.VMEM((HG_HEADS, HG_DK, HG_DK), F32),
                        pltpu.VMEM((MIX_ROWS, HG_WIDTH), F32),
                        pltpu.VMEM((MIX_ROWS, D), BF16),
                        pltpu.VMEM((HG_HEADS * (MIX_ROWS // HG_CHUNK), HG_CHUNK, HG_CHUNK), BF16)]
        + _weight_scratch([(D, in_width), (D, D)]),
        compiler_params=_params(2),
        name="mixer_ln2",
    )(h1, mat(w_in), hg_lb_logits.reshape(hg_lb_logits.shape[0], HG_WIDTH), row(hg_norm_g[0]),
      row(sg_ln_g[0]), row(sg_ln_b[0]), sg_w_s[0], sg_b_s[0].T, mat(w_out),
      row(ln2_g[0]), row(ln2_b[0]))

    row_spec = pl.BlockSpec((FFN_ROWS, D), lambda i: (i, 0))
    tiles_per_batch = T // FFN_ROWS
    kv_spec = pl.BlockSpec((MEM_LEN, D), lambda i: (i // tiles_per_batch, 0))
    out = pl.pallas_call(
        functools.partial(_xattn_ffn_kernel, alpha),
        grid=(n_rows // FFN_ROWS,),
        in_specs=[row_spec, kv_spec, kv_spec, _HBM, _HBM, _resident((1, D)), _resident((1, D)),
                  _HBM, _HBM, _HBM, _resident((1, D)), _resident((1, D))],
        out_specs=row_spec,
        out_shape=jax.ShapeDtypeStruct((n_rows, D), F32),
        scratch_shapes=[pltpu.VMEM((FFN_ROWS, D), BF16), pltpu.VMEM((FFN_ROWS, d_ff), BF16)]
        + _weight_scratch([(D, D), (D, D), (D, d_ff), (D, d_ff), (d_ff, D)]),
        compiler_params=_params(1),
        name="xattn_ln3_ffn2_ln4",
    )(h2, k_mem, v_mem, mat(xa_w_q), mat(xa_w_o), row(ln3_g[0]), row(ln3_b[0]),
      mat(ffn2_w_gate), mat(ffn2_w_up), mat(ffn2_w_down), row(ln4_g[0]), row(ln4_b[0]))

    return out.reshape(B, T, D)
```

```python
import functools

import jax
import jax.numpy as jnp
from jax import lax
from jax.experimental import pallas as pl
from jax.experimental.pallas import tpu as pltpu

F32 = jnp.float32
BF16 = jnp.bfloat16

LN_EPS = 1e-5
MEM_LEN = 256
HG_HEADS = 4
HG_DK = 128
HG_WIDTH = HG_HEADS * HG_DK
SG_GROUPS = 4
SG_DIM = 128
SG_WIDTH = SG_GROUPS * SG_DIM
SG_CHUNK = 128
X_HEADS = 4

HG_CHUNK = 64
HG_LEVELS = (1, 2, 4, 8, 16, 32)
HG_SAFE_LOG2 = 100.0

FFN1_ROWS = 1024
FFN_ROWS = 512
SUB_ROWS = 256
FFN_COLS = 256
MIX_ROWS = 512
CUMSUM_ROWS = 256
OUT_ROWS = 128
LOAD_ROWS = 128
LOAD_SLOTS = 4
VMEM_LIMIT_BYTES = 56 * 1024 * 1024


def _dot(a, b):
    return jnp.dot(a, b, preferred_element_type=F32)


def _dot_nt(a, b):
    return lax.dot_general(a, b, (((1,), (1,)), ((), ())), preferred_element_type=F32)


def _dot_tn(a, b):
    return lax.dot_general(a, b, (((0,), (0,)), ((), ())), preferred_element_type=F32)


def _layer_norm(y, g, b):
    mu = jnp.mean(y, axis=-1, keepdims=True)
    d = y - mu
    var = jnp.mean(d * d, axis=-1, keepdims=True)
    return d * lax.rsqrt(var + LN_EPS) * g + b


def _sigmoid(x):
    return 1.0 / (1.0 + jnp.exp(-x))


def _load_weights_bf16(pairs, stage, sem):
    chunks = [(src, dst, r) for src, dst in pairs for r in range(0, src.shape[0], LOAD_ROWS)]
    ahead = LOAD_SLOTS - 1

    def chunk_copy(i):
        src, _, r = chunks[i]
        slot = i % LOAD_SLOTS
        return pltpu.make_async_copy(src.at[pl.ds(r, LOAD_ROWS), :],
                                     stage.at[slot, :, pl.ds(0, src.shape[1])], sem.at[slot])

    for i in range(min(ahead, len(chunks))):
        chunk_copy(i).start()
    for i, (src, dst, r) in enumerate(chunks):
        if i + ahead < len(chunks):
            chunk_copy(i + ahead).start()
        chunk_copy(i).wait()
        dst[pl.ds(r, LOAD_ROWS), :] = stage[i % LOAD_SLOTS, :, 0:src.shape[1]].astype(BF16)


def _swiglu_ln(x, alpha, wg_ref, wu_ref, wd_ref, g_ref, b_ref, h_scr):
    xb = x.astype(BF16)
    d_ff = wg_ref.shape[1]
    for j in range(d_ff // FFN_COLS):
        sl = slice(j * FFN_COLS, (j + 1) * FFN_COLS)
        gate = _dot(xb, wg_ref[:, sl])
        up = _dot(xb, wu_ref[:, sl])
        h_scr[:, sl] = (gate * _sigmoid(gate) * up).astype(BF16)
    y = _dot(h_scr[...], wd_ref[...])
    return _layer_norm(alpha * x + 0.5 * y, g_ref[...], b_ref[...])


def _kv_kernel(mem_ref, g_ref, b_ref, wk_hbm, wv_hbm, k_ref, v_ref, wk_ref, wv_ref, stage, sem):
    @pl.when(pl.program_id(0) == 0)
    def _():
        _load_weights_bf16([(wk_hbm, wk_ref), (wv_hbm, wv_ref)], stage, sem)

    m = _layer_norm(mem_ref[...], g_ref[...], b_ref[...]).astype(BF16)
    k_ref[...] = _dot(m, wk_ref[...]).astype(BF16)
    v_ref[...] = _dot(m, wv_ref[...]).astype(BF16)


def _ffn_kernel(alpha, x_ref, wg_hbm, wu_hbm, wd_hbm, g_ref, b_ref, o_ref, h_scr,
                wg_ref, wu_ref, wd_ref, stage, sem):
    @pl.when(pl.program_id(0) == 0)
    def _():
        _load_weights_bf16([(wg_hbm, wg_ref), (wu_hbm, wu_ref), (wd_hbm, wd_ref)], stage, sem)

    for r in range(0, x_ref.shape[0], SUB_ROWS):
        rs = slice(r, r + SUB_ROWS)
        o_ref[rs, :] = _swiglu_ln(x_ref[rs, :], alpha, wg_ref, wu_ref, wd_ref, g_ref, b_ref, h_scr.at[rs, :])


def _bcast_row(a, blk, r):
    rows, w = a.shape
    a3 = a.reshape(rows // blk, blk, w)
    return jnp.broadcast_to(a3[:, r:r + 1, :], a3.shape).reshape(rows, w)


def _mixer_kernel(alpha, h_ref, w_in_hbm, lbl_ref, hgn_ref, sglg_ref, sglb_ref, ws_ref,
                  bst_ref, w_out_hbm, g_ref, b_ref, o_ref, st_ref, o_scr, mix_scr, sc_scr,
                  w_in_ref, w_out_ref, stage, sem):
    rows = h_ref.shape[0]
    W = HG_WIDTH

    @pl.when((pl.program_id(0) == 0) & (pl.program_id(1) == 0))
    def _():
        _load_weights_bf16([(w_in_hbm, w_in_ref), (w_out_hbm, w_out_ref)], stage, sem)

    @pl.when(pl.program_id(1) == 0)
    def _():
        st_ref[...] = jnp.zeros_like(st_ref)

    h = h_ref[...]
    proj = _dot(h.astype(BF16), w_in_ref[...])
    q = proj[:, 0:W]
    fz = proj[:, W:2 * W]
    iv = proj[:, 2 * W:3 * W]
    gt = proj[:, 3 * W:4 * W]
    uv = proj[:, 4 * W:]

    lbl = lbl_ref[...]
    lbe = jnp.exp(lbl - jnp.max(lbl, axis=0, keepdims=True))
    lb = lbe[0:1, :] / jnp.sum(lbe, axis=0, keepdims=True)

    e = jnp.exp(-jnp.abs(fz))
    r = 1.0 / (1.0 + e)
    er = e * r
    pos = fz >= 0.0
    f = lb + (1.0 - lb) * jnp.where(pos, r, er)
    k = (1.0 - lb) * jnp.where(pos, er, r)
    lf = jnp.log2(f)

    ri = lax.broadcasted_iota(jnp.int32, (CUMSUM_ROWS, CUMSUM_ROWS), 0)
    ci = lax.broadcasted_iota(jnp.int32, (CUMSUM_ROWS, CUMSUM_ROWS), 1)
    tri = ((ri >= ci) & (ri // HG_CHUNK == ci // HG_CHUNK)).astype(BF16)
    lf0 = lf.astype(BF16)
    rem = lf - lf0.astype(F32)
    lf1 = rem.astype(BF16)
    lf2 = (rem - lf1.astype(F32)).astype(BF16)
    b = jnp.concatenate(
        [_dot(tri, lf0[rs, :]) + _dot(tri, lf1[rs, :]) + _dot(tri, lf2[rs, :])
         for rs in (slice(i, i + CUMSUM_ROWS) for i in range(0, rows, CUMSUM_ROWS))], axis=0)

    rw = lax.broadcasted_iota(jnp.int32, (rows, W), 0)
    vb16 = iv.astype(BF16)
    n_chunks = rows // HG_CHUNK
    blocks = [(hh, c) for hh in range(HG_HEADS) for c in range(n_chunks)]
    col = lambda hh: slice(hh * HG_DK, (hh + 1) * HG_DK)
    row_ = lambda c: slice(c * HG_CHUNK, (c + 1) * HG_CHUNK)
    ti = lax.broadcasted_iota(jnp.int32, (HG_CHUNK, HG_CHUNK), 0)
    si = lax.broadcasted_iota(jnp.int32, (HG_CHUNK, HG_CHUNK), 1)

    b_end = _bcast_row(b, HG_CHUNK, HG_CHUNK - 1)
    q_st = (q * jnp.exp2(b)).astype(BF16)
    k_st = (k * jnp.exp2(b_end - b)).astype(BF16)
    dec_end = jnp.exp2(b_end)
    st_inc = {(hh, c): _dot_tn(vb16[row_(c), col(hh)], k_st[row_(c), col(hh)]) for hh, c in blocks}
    gate_act = gt * _sigmoid(gt)

    uvg = uv * (0.5 * (1.0 + jnp.tanh(0.7978845608028654 * (uv + 0.044715 * (uv * uv * uv)))))
    causal = (lax.broadcasted_iota(jnp.int32, (SG_CHUNK, SG_CHUNK), 0)
              >= lax.broadcasted_iota(jnp.int32, (SG_CHUNK, SG_CHUNK), 1))
    for g in range(SG_GROUPS):
        cs = slice(g * SG_DIM, (g + 1) * SG_DIM)
        u_g = uvg[:, cs]
        v_g = uvg[:, SG_WIDTH + g * SG_DIM:SG_WIDTH + (g + 1) * SG_DIM]
        vn = _layer_norm(v_g, sglg_ref[:, cs], sglb_ref[:, cs]).astype(BF16)
        w_g = jnp.where(causal, ws_ref[g], 0.0).astype(BF16)
        bias = bst_ref[:, g:g + 1]
        for n in range(rows // SG_CHUNK):
            rs = slice(n * SG_CHUNK, (n + 1) * SG_CHUNK)
            s_ = _dot(w_g, vn[rs, :]) + bias
            mix_scr[rs, HG_WIDTH + g * SG_DIM:HG_WIDTH + (g + 1) * SG_DIM] = (u_g[rs, :] * s_).astype(BF16)

    safe = jnp.max(-b_end) < HG_SAFE_LOG2

    @pl.when(safe)
    def _():
        x = b - _bcast_row(b, HG_CHUNK, HG_CHUNK // 2 - 1)
        qm = (q * jnp.exp2(x)).astype(BF16)
        km = (k * jnp.exp2(-x)).astype(BF16)
        for i, (hh, c) in enumerate(blocks):
            s_ = _dot_nt(qm[row_(c), col(hh)], km[row_(c), col(hh)])
            sc_scr[i] = jnp.where(ti >= si, s_, 0.0).astype(BF16)

    @pl.when(jnp.logical_not(safe))
    def _():
        lhs = [q.astype(BF16), jnp.where((rw % 2) == 1, q * f, k).astype(BF16)]
        for L in HG_LEVELS[1:]:
            if 2 * L >= 8:
                ref = _bcast_row(b, 2 * L, L - 1)
            else:
                ref = jnp.where((rw % 8) < 4, _bcast_row(b, 8, 1), _bcast_row(b, 8, 5))
            el = jnp.exp2(-jnp.abs(b - ref))
            upper = (rw % (2 * L)) >= L
            lhs.append((jnp.where(upper, q, k) * el).astype(BF16))
        rhs = [k.astype(BF16)] + lhs[1:]
        masks = [ti == si] + [((ti // (2 * L)) == (si // (2 * L))) & ((ti % (2 * L)) >= L) & ((si % (2 * L)) < L)
                              for L in HG_LEVELS]
        for i, (hh, c) in enumerate(blocks):
            s_ = jnp.zeros((HG_CHUNK, HG_CHUNK), F32)
            for a_, b_, m_ in zip(lhs, rhs, masks):
                s_ = jnp.where(m_, _dot_nt(a_[row_(c), col(hh)], b_[row_(c), col(hh)]), s_)
            sc_scr[i] = s_.astype(BF16)

    for hh in range(HG_HEADS):
        st = st_ref[hh]
        for c in range(n_chunks):
            o_scr[row_(c), col(hh)] = (_dot(sc_scr[hh * n_chunks + c], vb16[row_(c), col(hh)])
                                       + _dot_nt(q_st[row_(c), col(hh)], st.astype(BF16)))
            last = (c + 1) * HG_CHUNK - 1
            st = st * dec_end[last:last + 1, col(hh)] + st_inc[hh, c]
        st_ref[hh] = st

    o = o_scr[...]
    hgn = hgn_ref[...]
    for hh in range(HG_HEADS):
        cs = col(hh)
        oh = o[:, cs]
        ms = jnp.mean(oh * oh, axis=-1, keepdims=True)
        mix_scr[:, cs] = (oh * lax.rsqrt(ms + LN_EPS) * hgn * gate_act[:, cs]).astype(BF16)

    for r in range(0, rows, OUT_ROWS):
        rs = slice(r, r + OUT_ROWS)
        mix = _dot(mix_scr[rs, :], w_out_ref[...])
        o_ref[rs, :] = _layer_norm(alpha * h[rs, :] + mix, g_ref[...], b_ref[...])


def _xattn_ffn_kernel(alpha, h_ref, k_ref, v_ref, wq_hbm, wo_hbm, g3_ref, b3_ref,
                      wg_hbm, wu_hbm, wd_hbm, g4_ref, b4_ref, o_ref, a_scr, h_scr,
                      wq_ref, wo_ref, wg_ref, wu_ref, wd_ref, stage, sem):
    @pl.when(pl.program_id(0) == 0)
    def _():
        _load_weights_bf16([(wq_hbm, wq_ref), (wo_hbm, wo_ref), (wg_hbm, wg_ref), (wu_hbm, wu_ref),
                            (wd_hbm, wd_ref)], stage, sem)

    h = h_ref[...]
    d_model = h.shape[1]
    hd = d_model // X_HEADS
    qb = (_dot(h.astype(BF16), wq_ref[...]) * (hd ** -0.5)).astype(BF16)
    for hh in range(X_HEADS):
        cs = slice(hh * hd, (hh + 1) * hd)
        s = _dot_nt(qb[:, cs], k_ref[:, cs])
        p = jnp.exp(s - jnp.max(s, axis=-1, keepdims=True))
        l = jnp.sum(p, axis=-1, keepdims=True)
        a_scr[:, cs] = (_dot(p.astype(BF16), v_ref[:, cs]) / l).astype(BF16)
    xa = _dot(a_scr[...], wo_ref[...])
    h3 = _layer_norm(alpha * h + xa, g3_ref[...], b3_ref[...])
    for r in range(0, h3.shape[0], SUB_ROWS):
        rs = slice(r, r + SUB_ROWS)
        o_ref[rs, :] = _swiglu_ln(h3[rs, :], alpha, wg_ref, wu_ref, wd_ref, g4_ref, b4_ref, h_scr.at[rs, :])


def _resident(shape):
    return pl.BlockSpec(shape, lambda *_: (0,) * len(shape), pipeline_mode=pl.Buffered(1))


_HBM = pl.BlockSpec(memory_space=pl.ANY)


def _weight_scratch(shapes):
    width = max(c for _, c in shapes)
    return ([pltpu.VMEM(shape, BF16) for shape in shapes]
            + [pltpu.VMEM((LOAD_SLOTS, LOAD_ROWS, width), F32), pltpu.SemaphoreType.DMA((LOAD_SLOTS,))])


def _params(n_grid_axes):
    return pltpu.CompilerParams(dimension_semantics=("arbitrary",) * n_grid_axes,
                                vmem_limit_bytes=VMEM_LIMIT_BYTES)


def kernel(x, mem, ffn1_w_gate, ffn1_w_up, ffn1_w_down, ln1_g, ln1_b, w_in, hg_lb_logits, hg_norm_g, sg_ln_g, sg_ln_b, sg_w_s, sg_b_s, w_out, ln2_g, ln2_b, mem_ln_g, mem_ln_b, xa_w_q, xa_w_k, xa_w_v, xa_w_o, ln3_g, ln3_b, ffn2_w_gate, ffn2_w_up, ffn2_w_down, ln4_g, ln4_b):
    depth = w_in.shape[0]
    assert depth == 1, "single-layer stack"
    B, T, D = x.shape
    n_rows = B * T
    d_ff = ffn1_w_gate.shape[2]
    alpha = (2.0 * depth) ** 0.25
    assert T % MIX_ROWS == 0 and T % FFN_ROWS == 0 and T % FFN1_ROWS == 0 and d_ff % FFN_COLS == 0
    assert MIX_ROWS % SG_CHUNK == 0 and MIX_ROWS % HG_CHUNK == 0
    assert D % LOAD_ROWS == 0 and d_ff % LOAD_ROWS == 0

    row = lambda a: a.reshape(1, -1)
    mat = lambda a: a.reshape(a.shape[1:])
    x2 = x.reshape(n_rows, D)
    mem2 = mem.reshape(B * MEM_LEN, D)

    k_mem, v_mem = pl.pallas_call(
        _kv_kernel,
        grid=(B,),
        in_specs=[pl.BlockSpec((MEM_LEN, D), lambda i: (i, 0)),
                  _resident((1, D)), _resident((1, D)), _HBM, _HBM],
        out_specs=[pl.BlockSpec((MEM_LEN, D), lambda i: (i, 0))] * 2,
        out_shape=[jax.ShapeDtypeStruct((B * MEM_LEN, D), BF16)] * 2,
        scratch_shapes=_weight_scratch([(D, D), (D, D)]),
        compiler_params=_params(1),
        name="memory_kv",
    )(mem2, row(mem_ln_g[0]), row(mem_ln_b[0]), mat(xa_w_k), mat(xa_w_v))

    row1_spec = pl.BlockSpec((FFN1_ROWS, D), lambda i: (i, 0))
    h1 = pl.pallas_call(
        functools.partial(_ffn_kernel, alpha),
        grid=(n_rows // FFN1_ROWS,),
        in_specs=[row1_spec, _HBM, _HBM, _HBM, _resident((1, D)), _resident((1, D))],
        out_specs=row1_spec,
        out_shape=jax.ShapeDtypeStruct((n_rows, D), F32),
        scratch_shapes=[pltpu.VMEM((FFN1_ROWS, d_ff), BF16)]
        + _weight_scratch([(D, d_ff), (D, d_ff), (d_ff, D)]),
        compiler_params=_params(1),
        name="ffn1_ln1",
    )(x2, mat(ffn1_w_gate), mat(ffn1_w_up), mat(ffn1_w_down), row(ln1_g[0]), row(ln1_b[0]))

    tiles_per_seq = T // MIX_ROWS
    in_width = w_in.shape[2]
    mix_spec = pl.BlockSpec((MIX_ROWS, D), lambda bi, ti: (bi * tiles_per_seq + ti, 0))
    h2 = pl.pallas_call(
        functools.partial(_mixer_kernel, alpha),
        grid=(B, tiles_per_seq),
        in_specs=[mix_spec, _HBM,
                  _resident((hg_lb_logits.shape[0], HG_WIDTH)), _resident((1, HG_DK)),
                  _resident((1, SG_WIDTH)), _resident((1, SG_WIDTH)),
                  _resident((SG_GROUPS, SG_CHUNK, SG_CHUNK)), _resident((SG_CHUNK, SG_GROUPS)),
                  _HBM, _resident((1, D)), _resident((1, D))],
        out_specs=mix_spec,
        out_shape=jax.ShapeDtypeStruct((n_rows, D), F32),
        scratch_shapes=[pltpu.VMEM((HG_HEADS, HG_DK, HG_DK), F32),
                        pltpu.VMEM((MIX_ROWS, HG_WIDTH), F32),
                        pltpu.VMEM((MIX_ROWS, D), BF16),
                        pltpu.VMEM((HG_HEADS * (MIX_ROWS // HG_CHUNK), HG_CHUNK, HG_CHUNK), BF16)]
        + _weight_scratch([(D, in_width), (D, D)]),
        compiler_params=_params(2),
        name="mixer_ln2",
    )(h1, mat(w_in), hg_lb_logits.reshape(hg_lb_logits.shape[0], HG_WIDTH), row(hg_norm_g[0]),
      row(sg_ln_g[0]), row(sg_ln_b[0]), sg_w_s[0], sg_b_s[0].T, mat(w_out),
      row(ln2_g[0]), row(ln2_b[0]))

    row_spec = pl.BlockSpec((FFN_ROWS, D), lambda i: (i, 0))
    tiles_per_batch = T // FFN_ROWS
    kv_spec = pl.BlockSpec((MEM_LEN, D), lambda i: (i // tiles_per_batch, 0))
    out = pl.pallas_call(
        functools.partial(_xattn_ffn_kernel, alpha),
        grid=(n_rows // FFN_ROWS,),
        in_specs=[row_spec, kv_spec, kv_spec, _HBM, _HBM, _resident((1, D)), _resident((1, D)),
                  _HBM, _HBM, _HBM, _resident((1, D)), _resident((1, D))],
        out_specs=row_spec,
        out_shape=jax.ShapeDtypeStruct((n_rows, D), F32),
        scratch_shapes=[pltpu.VMEM((FFN_ROWS, D), BF16), pltpu.VMEM((FFN_ROWS, d_ff), BF16)]
        + _weight_scratch([(D, D), (D, D), (D, d_ff), (D, d_ff), (d_ff, D)]),
        compiler_params=_params(1),
        name="xattn_ln3_ffn2_ln4",
    )(h2, k_mem, v_mem, mat(xa_w_q), mat(xa_w_o), row(ln3_g[0]), row(ln3_b[0]),
      mat(ffn2_w_gate), mat(ffn2_w_up), mat(ffn2_w_down), row(ln4_g[0]), row(ln4_b[0]))

    return out.reshape(B, T, D)
```

```python
import functools

import jax
import jax.numpy as jnp
from jax import lax
from jax.experimental import pallas as pl
from jax.experimental.pallas import tpu as pltpu

F32 = jnp.float32
BF16 = jnp.bfloat16

LN_EPS = 1e-5
MEM_LEN = 256
HG_HEADS = 4
HG_DK = 128
HG_WIDTH = HG_HEADS * HG_DK
SG_GROUPS = 4
SG_DIM = 128
SG_WIDTH = SG_GROUPS * SG_DIM
SG_CHUNK = 128
X_HEADS = 4

HG_CHUNK = 64
HG_LEVELS = (1, 2, 4, 8, 16, 32)
HG_SAFE_LOG2 = 100.0

FFN1_ROWS = 1024
FFN_ROWS = 512
SUB_ROWS = 256
FFN_COLS = 256
MIX_ROWS = 512
CUMSUM_ROWS = 256
OUT_ROWS = 128
LOAD_ROWS = 128
LOAD_SLOTS = 4
VMEM_LIMIT_BYTES = 56 * 1024 * 1024


def _dot(a, b):
    return jnp.dot(a, b, preferred_element_type=F32)


def _dot_nt(a, b):
    return lax.dot_general(a, b, (((1,), (1,)), ((), ())), preferred_element_type=F32)


def _dot_tn(a, b):
    return lax.dot_general(a, b, (((0,), (0,)), ((), ())), preferred_element_type=F32)


def _layer_norm(y, g, b):
    mu = jnp.mean(y, axis=-1, keepdims=True)
    d = y - mu
    var = jnp.mean(d * d, axis=-1, keepdims=True)
    return d * lax.rsqrt(var + LN_EPS) * g + b


def _sigmoid(x):
    return 1.0 / (1.0 + jnp.exp(-x))


def _load_weights_bf16(pairs, stage, sem):
    chunks = [(src, dst, r) for src, dst in pairs for r in range(0, src.shape[0], LOAD_ROWS)]
    ahead = LOAD_SLOTS - 1

    def chunk_copy(i):
        src, _, r = chunks[i]
        slot = i % LOAD_SLOTS
        return pltpu.make_async_copy(src.at[pl.ds(r, LOAD_ROWS), :],
                                     stage.at[slot, :, pl.ds(0, src.shape[1])], sem.at[slot])

    for i in range(min(ahead, len(chunks))):
        chunk_copy(i).start()
    for i, (src, dst, r) in enumerate(chunks):
        if i + ahead < len(chunks):
            chunk_copy(i + ahead).start()
        chunk_copy(i).wait()
        dst[pl.ds(r, LOAD_ROWS), :] = stage[i % LOAD_SLOTS, :, 0:src.shape[1]].astype(BF16)


def _swiglu_ln(x, alpha, wg_ref, wu_ref, wd_ref, g_ref, b_ref, h_scr):
    xb = x.astype(BF16)
    d_ff = wg_ref.shape[1]
    for j in range(d_ff // FFN_COLS):
        sl = slice(j * FFN_COLS, (j + 1) * FFN_COLS)
        gate = _dot(xb, wg_ref[:, sl])
        up = _dot(xb, wu_ref[:, sl])
        h_scr[:, sl] = (gate * _sigmoid(gate) * up).astype(BF16)
    y = _dot(h_scr[...], wd_ref[...])
    return _layer_norm(alpha * x + 0.5 * y, g_ref[...], b_ref[...])


def _kv_kernel(mem_ref, g_ref, b_ref, wk_hbm, wv_hbm, k_ref, v_ref, wk_ref, wv_ref, stage, sem):
    @pl.when(pl.program_id(0) == 0)
    def _():
        _load_weights_bf16([(wk_hbm, wk_ref), (wv_hbm, wv_ref)], stage, sem)

    m = _layer_norm(mem_ref[...], g_ref[...], b_ref[...]).astype(BF16)
    k_ref[...] = _dot(m, wk_ref[...]).astype(BF16)
    v_ref[...] = _dot(m, wv_ref[...]).astype(BF16)


def _ffn_kernel(alpha, x_ref, wg_hbm, wu_hbm, wd_hbm, g_ref, b_ref, o_ref, h_scr,
                wg_ref, wu_ref, wd_ref, stage, sem):
    @pl.when(pl.program_id(0) == 0)
    def _():
        _load_weights_bf16([(wg_hbm, wg_ref), (wu_hbm, wu_ref), (wd_hbm, wd_ref)], stage, sem)

    for r in range(0, x_ref.shape[0], SUB_ROWS):
        rs = slice(r, r + SUB_ROWS)
        o_ref[rs, :] = _swiglu_ln(x_ref[rs, :], alpha, wg_ref, wu_ref, wd_ref, g_ref, b_ref, h_scr.at[rs, :])


def _bcast_row(a, blk, r):
    rows, w = a.shape
    a3 = a.reshape(rows // blk, blk, w)
    return jnp.broadcast_to(a3[:, r:r + 1, :], a3.shape).reshape(rows, w)


def _mixer_kernel(alpha, h_ref, w_in_hbm, lbl_ref, hgn_ref, sglg_ref, sglb_ref, ws_ref,
                  bst_ref, w_out_hbm, g_ref, b_ref, o_ref, st_ref, o_scr, mix_scr, sc_scr,
                  w_in_ref, w_out_ref, stage, sem):
    rows = h_ref.shape[0]
    W = HG_WIDTH

    @pl.when((pl.program_id(0) == 0) & (pl.program_id(1) == 0))
    def _():
        _load_weights_bf16([(w_in_hbm, w_in_ref), (w_out_hbm, w_out_ref)], stage, sem)

    @pl.when(pl.program_id(1) == 0)
    def _():
        st_ref[...] = jnp.zeros_like(st_ref)

    h = h_ref[...]
    proj = _dot(h.astype(BF16), w_in_ref[...])
    q = proj[:, 0:W]
    fz = proj[:, W:2 * W]
    iv = proj[:, 2 * W:3 * W]
    gt = proj[:, 3 * W:4 * W]
    uv = proj[:, 4 * W:]

    lbl = lbl_ref[...]
    lbe = jnp.exp(lbl - jnp.max(lbl, axis=0, keepdims=True))
    lb = lbe[0:1, :] / jnp.sum(lbe, axis=0, keepdims=True)

    e = jnp.exp(-jnp.abs(fz))
    r = 1.0 / (1.0 + e)
    er = e * r
    pos = fz >= 0.0
    f = lb + (1.0 - lb) * jnp.where(pos, r, er)
    k = (1.0 - lb) * jnp.where(pos, er, r)
    lf = jnp.log2(f)

    ri = lax.broadcasted_iota(jnp.int32, (CUMSUM_ROWS, CUMSUM_ROWS), 0)
    ci = lax.broadcasted_iota(jnp.int32, (CUMSUM_ROWS, CUMSUM_ROWS), 1)
    tri = ((ri >= ci) & (ri // HG_CHUNK == ci // HG_CHUNK)).astype(BF16)
    lf0 = lf.astype(BF16)
    rem = lf - lf0.astype(F32)
    lf1 = rem.astype(BF16)
    lf2 = (rem - lf1.astype(F32)).astype(BF16)
    b = jnp.concatenate(
        [_dot(tri, lf0[rs, :]) + _dot(tri, lf1[rs, :]) + _dot(tri, lf2[rs, :])
         for rs in (slice(i, i + CUMSUM_ROWS) for i in range(0, rows, CUMSUM_ROWS))], axis=0)

    rw = lax.broadcasted_iota(jnp.int32, (rows, W), 0)
    vb16 = iv.astype(BF16)
    n_chunks = rows // HG_CHUNK
    blocks = [(hh, c) for hh in range(HG_HEADS) for c in range(n_chunks)]
    col = lambda hh: slice(hh * HG_DK, (hh + 1) * HG_DK)
    row_ = lambda c: slice(c * HG_CHUNK, (c + 1) * HG_CHUNK)
    ti = lax.broadcasted_iota(jnp.int32, (HG_CHUNK, HG_CHUNK), 0)
    si = lax.broadcasted_iota(jnp.int32, (HG_CHUNK, HG_CHUNK), 1)

    b_end = _bcast_row(b, HG_CHUNK, HG_CHUNK - 1)
    q_st = (q * jnp.exp2(b)).astype(BF16)
    k_st = (k * jnp.exp2(b_end - b)).astype(BF16)
    dec_end = jnp.exp2(b_end)
    st_inc = {(hh, c): _dot_tn(vb16[row_(c), col(hh)], k_st[row_(c), col(hh)]) for hh, c in blocks}
    gate_act = gt * _sigmoid(gt)

    uvg = uv * (0.5 * (1.0 + jnp.tanh(0.7978845608028654 * (uv + 0.044715 * (uv * uv * uv)))))
    causal = (lax.broadcasted_iota(jnp.int32, (SG_CHUNK, SG_CHUNK), 0)
              >= lax.broadcasted_iota(jnp.int32, (SG_CHUNK, SG_CHUNK), 1))
    for g in range(SG_GROUPS):
        cs = slice(g * SG_DIM, (g + 1) * SG_DIM)
        u_g = uvg[:, cs]
        v_g = uvg[:, SG_WIDTH + g * SG_DIM:SG_WIDTH + (g + 1) * SG_DIM]
        vn = _layer_norm(v_g, sglg_ref[:, cs], sglb_ref[:, cs]).astype(BF16)
        w_g = jnp.where(causal, ws_ref[g], 0.0).astype(BF16)
        bias = bst_ref[:, g:g + 1]
        for n in range(rows // SG_CHUNK):
            rs = slice(n * SG_CHUNK, (n + 1) * SG_CHUNK)
            s_ = _dot(w_g, vn[rs, :]) + bias
            mix_scr[rs, HG_WIDTH + g * SG_DIM:HG_WIDTH + (g + 1) * SG_DIM] = (u_g[rs, :] * s_).astype(BF16)

    x = b - _bcast_row(b, HG_CHUNK, HG_CHUNK // 2 - 1)
    qm = (q * jnp.exp2(x)).astype(BF16)
    km = (k * jnp.exp2(-x)).astype(BF16)
    for i, (hh, c) in enumerate(blocks):
        s_ = _dot_nt(qm[row_(c), col(hh)], km[row_(c), col(hh)])
        sc_scr[i] = jnp.where(ti >= si, s_, 0.0).astype(BF16)
    chunk_ends = b.reshape(n_chunks, HG_CHUNK, W)[:, HG_CHUNK - 1, :]
    safe = jnp.max(-chunk_ends) < HG_SAFE_LOG2

    @pl.when(jnp.logical_not(safe))
    def _():
        lhs = [q.astype(BF16), jnp.where((rw % 2) == 1, q * f, k).astype(BF16)]
        for L in HG_LEVELS[1:]:
            if 2 * L >= 8:
                ref = _bcast_row(b, 2 * L, L - 1)
            else:
                ref = jnp.where((rw % 8) < 4, _bcast_row(b, 8, 1), _bcast_row(b, 8, 5))
            el = jnp.exp2(-jnp.abs(b - ref))
            upper = (rw % (2 * L)) >= L
            lhs.append((jnp.where(upper, q, k) * el).astype(BF16))
        rhs = [k.astype(BF16)] + lhs[1:]
        masks = [ti == si] + [((ti // (2 * L)) == (si // (2 * L))) & ((ti % (2 * L)) >= L) & ((si % (2 * L)) < L)
                              for L in HG_LEVELS]
        for i, (hh, c) in enumerate(blocks):
            s_ = jnp.zeros((HG_CHUNK, HG_CHUNK), F32)
            for a_, b_, m_ in zip(lhs, rhs, masks):
                s_ = jnp.where(m_, _dot_nt(a_[row_(c), col(hh)], b_[row_(c), col(hh)]), s_)
            sc_scr[i] = s_.astype(BF16)

    for hh in range(HG_HEADS):
        st = st_ref[hh]
        for c in range(n_chunks):
            o_scr[row_(c), col(hh)] = (_dot(sc_scr[hh * n_chunks + c], vb16[row_(c), col(hh)])
                                       + _dot_nt(q_st[row_(c), col(hh)], st.astype(BF16)))
            last = (c + 1) * HG_CHUNK - 1
            st = st * dec_end[last:last + 1, col(hh)] + st_inc[hh, c]
        st_ref[hh] = st

    o = o_scr[...]
    hgn = hgn_ref[...]
    for hh in range(HG_HEADS):
        cs = col(hh)
        oh = o[:, cs]
        ms = jnp.mean(oh * oh, axis=-1, keepdims=True)
        mix_scr[:, cs] = (oh * lax.rsqrt(ms + LN_EPS) * hgn * gate_act[:, cs]).astype(BF16)

    for r in range(0, rows, OUT_ROWS):
        rs = slice(r, r + OUT_ROWS)
        mix = _dot(mix_scr[rs, :], w_out_ref[...])
        o_ref[rs, :] = _layer_norm(alpha * h[rs, :] + mix, g_ref[...], b_ref[...])


def _xattn_ffn_kernel(alpha, h_ref, k_ref, v_ref, wq_hbm, wo_hbm, g3_ref, b3_ref,
                      wg_hbm, wu_hbm, wd_hbm, g4_ref, b4_ref, o_ref, a_scr, h_scr,
                      wq_ref, wo_ref, wg_ref, wu_ref, wd_ref, stage, sem):
    @pl.when(pl.program_id(0) == 0)
    def _():
        _load_weights_bf16([(wq_hbm, wq_ref), (wo_hbm, wo_ref), (wg_hbm, wg_ref), (wu_hbm, wu_ref),
                            (wd_hbm, wd_ref)], stage, sem)

    h = h_ref[...]
    d_model = h.shape[1]
    hd = d_model // X_HEADS
    qb = (_dot(h.astype(BF16), wq_ref[...]) * (hd ** -0.5)).astype(BF16)
    for hh in range(X_HEADS):
        cs = slice(hh * hd, (hh + 1) * hd)
        s = _dot_nt(qb[:, cs], k_ref[:, cs])
        p = jnp.exp(s - jnp.max(s, axis=-1, keepdims=True))
        l = jnp.sum(p, axis=-1, keepdims=True)
        a_scr[:, cs] = (_dot(p.astype(BF16), v_ref[:, cs]) / l).astype(BF16)
    groups = [slice(r, r + SUB_ROWS) for r in range(0, h.shape[0], SUB_ROWS)]
    h3 = [_layer_norm(alpha * h[rs, :] + _dot(a_scr[rs, :], wo_ref[...]), g3_ref[...], b3_ref[...])
          for rs in groups]
    for rs, h3_g in zip(groups, h3):
        o_ref[rs, :] = _swiglu_ln(h3_g, alpha, wg_ref, wu_ref, wd_ref, g4_ref, b4_ref, h_scr.at[rs, :])


def _resident(shape):
    return pl.BlockSpec(shape, lambda *_: (0,) * len(shape), pipeline_mode=pl.Buffered(1))


_HBM = pl.BlockSpec(memory_space=pl.ANY)


def _weight_scratch(shapes):
    width = max(c for _, c in shapes)
    return ([pltpu.VMEM(shape, BF16) for shape in shapes]
            + [pltpu.VMEM((LOAD_SLOTS, LOAD_ROWS, width), F32), pltpu.SemaphoreType.DMA((LOAD_SLOTS,))])


def _params(n_grid_axes):
    return pltpu.CompilerParams(dimension_semantics=("arbitrary",) * n_grid_axes,
                                vmem_limit_bytes=VMEM_LIMIT_BYTES)


def kernel(x, mem, ffn1_w_gate, ffn1_w_up, ffn1_w_down, ln1_g, ln1_b, w_in, hg_lb_logits, hg_norm_g, sg_ln_g, sg_ln_b, sg_w_s, sg_b_s, w_out, ln2_g, ln2_b, mem_ln_g, mem_ln_b, xa_w_q, xa_w_k, xa_w_v, xa_w_o, ln3_g, ln3_b, ffn2_w_gate, ffn2_w_up, ffn2_w_down, ln4_g, ln4_b):
    depth = w_in.shape[0]
    assert depth == 1, "single-layer stack"
    B, T, D = x.shape
    n_rows = B * T
    d_ff = ffn1_w_gate.shape[2]
    alpha = (2.0 * depth) ** 0.25
    assert T % MIX_ROWS == 0 and T % FFN_ROWS == 0 and T % FFN1_ROWS == 0 and d_ff % FFN_COLS == 0
    assert MIX_ROWS % SG_CHUNK == 0 and MIX_ROWS % HG_CHUNK == 0
    assert D % LOAD_ROWS == 0 and d_ff % LOAD_ROWS == 0

    row = lambda a: a.reshape(1, -1)
    mat = lambda a: a.reshape(a.shape[1:])
    x2 = x.reshape(n_rows, D)
    mem2 = mem.reshape(B * MEM_LEN, D)

    k_mem, v_mem = pl.pallas_call(
        _kv_kernel,
        grid=(B,),
        in_specs=[pl.BlockSpec((MEM_LEN, D), lambda i: (i, 0)),
                  _resident((1, D)), _resident((1, D)), _HBM, _HBM],
        out_specs=[pl.BlockSpec((MEM_LEN, D), lambda i: (i, 0))] * 2,
        out_shape=[jax.ShapeDtypeStruct((B * MEM_LEN, D), BF16)] * 2,
        scratch_shapes=_weight_scratch([(D, D), (D, D)]),
        compiler_params=_params(1),
        name="memory_kv",
    )(mem2, row(mem_ln_g[0]), row(mem_ln_b[0]), mat(xa_w_k), mat(xa_w_v))

    row1_spec = pl.BlockSpec((FFN1_ROWS, D), lambda i: (i, 0))
    h1 = pl.pallas_call(
        functools.partial(_ffn_kernel, alpha),
        grid=(n_rows // FFN1_ROWS,),
        in_specs=[row1_spec, _HBM, _HBM, _HBM, _resident((1, D)), _resident((1, D))],
        out_specs=row1_spec,
        out_shape=jax.ShapeDtypeStruct((n_rows, D), F32),
        scratch_shapes=[pltpu.VMEM((FFN1_ROWS, d_ff), BF16)]
        + _weight_scratch([(D, d_ff), (D, d_ff), (d_ff, D)]),
        compiler_params=_params(1),
        name="ffn1_ln1",
    )(x2, mat(ffn1_w_gate), mat(ffn1_w_up), mat(ffn1_w_down), row(ln1_g[0]), row(ln1_b[0]))

    tiles_per_seq = T // MIX_ROWS
    in_width = w_in.shape[2]
    mix_spec = pl.BlockSpec((MIX_ROWS, D), lambda bi, ti: (bi * tiles_per_seq + ti, 0))
    h2 = pl.pallas_call(
        functools.partial(_mixer_kernel, alpha),
        grid=(B, tiles_per_seq),
        in_specs=[mix_spec, _HBM,
                  _resident((hg_lb_logits.shape[0], HG_WIDTH)), _resident((1, HG_DK)),
                  _resident((1, SG_WIDTH)), _resident((1, SG_WIDTH)),
                  _resident((SG_GROUPS, SG_CHUNK, SG_CHUNK)), _resident((SG_CHUNK, SG_GROUPS)),
                  _HBM, _resident((1, D)), _resident((1, D))],
        out_specs=mix_spec,
        out_shape=jax.ShapeDtypeStruct((n_rows, D), F32),
        scratch_shapes=[pltpu.VMEM((HG_HEADS, HG_DK, HG_DK), F32),
                        pltpu.VMEM((MIX_ROWS, HG_WIDTH), F32),
                        pltpu.VMEM((MIX_ROWS, D), BF16),
                        pltpu.VMEM((HG_HEADS * (MIX_ROWS // HG_CHUNK), HG_CHUNK, HG_CHUNK), BF16)]
        + _weight_scratch([(D, in_width), (D, D)]),
        compiler_params=_params(2),
        name="mixer_ln2",
    )(h1, mat(w_in), hg_lb_logits.reshape(hg_lb_logits.shape[0], HG_WIDTH), row(hg_norm_g[0]),
      row(sg_ln_g[0]), row(sg_ln_b[0]), sg_w_s[0], sg_b_s[0].T, mat(w_out),
      row(ln2_g[0]), row(ln2_b[0]))

    row_spec = pl.BlockSpec((FFN_ROWS, D), lambda i: (i, 0))
    tiles_per_batch = T // FFN_ROWS
    kv_spec = pl.BlockSpec((MEM_LEN, D), lambda i: (i // tiles_per_batch, 0))
    out = pl.pallas_call(
        functools.partial(_xattn_ffn_kernel, alpha),
        grid=(n_rows // FFN_ROWS,),
        in_specs=[row_spec, kv_spec, kv_spec, _HBM, _HBM, _resident((1, D)), _resident((1, D)),
                  _HBM, _HBM, _HBM, _resident((1, D)), _resident((1, D))],
        out_specs=row_spec,
        out_shape=jax.ShapeDtypeStruct((n_rows, D), F32),
        scratch_shapes=[pltpu.VMEM((FFN_ROWS, D), BF16), pltpu.VMEM((FFN_ROWS, d_ff), BF16)]
        + _weight_scratch([(D, D), (D, D), (D, d_ff), (D, d_ff), (d_ff, D)]),
        compiler_params=_params(1),
        name="xattn_ln3_ffn2_ln4",
    )(h2, k_mem, v_mem, mat(xa_w_q), mat(xa_w_o), row(ln3_g[0]), row(ln3_b[0]),
      mat(ffn2_w_gate), mat(ffn2_w_up), mat(ffn2_w_down), row(ln4_g[0]), row(ln4_b[0]))

    return out.reshape(B, T, D)
```

```python
import functools

import jax
import jax.numpy as jnp
from jax import lax
from jax.experimental import pallas as pl
from jax.experimental.pallas import tpu as pltpu

F32 = jnp.float32
BF16 = jnp.bfloat16

LN_EPS = 1e-5
MEM_LEN = 256
HG_HEADS = 4
HG_DK = 128
HG_WIDTH = HG_HEADS * HG_DK
SG_GROUPS = 4
SG_DIM = 128
SG_WIDTH = SG_GROUPS * SG_DIM
SG_CHUNK = 128
X_HEADS = 4

HG_CHUNK = 64
HG_LEVELS = (1, 2, 4, 8, 16, 32)
HG_SAFE_LOG2 = 100.0

FFN1_ROWS = 1024
FFN_ROWS = 512
SUB_ROWS = 256
FFN_COLS = 256
MIX_ROWS = 512
CUMSUM_ROWS = 256
OUT_ROWS = 128
LOAD_ROWS = 64
LOAD_SLOTS = 8
VMEM_LIMIT_BYTES = 56 * 1024 * 1024


def _dot(a, b):
    return jnp.dot(a, b, preferred_element_type=F32)


def _dot_nt(a, b):
    return lax.dot_general(a, b, (((1,), (1,)), ((), ())), preferred_element_type=F32)


def _dot_tn(a, b):
    return lax.dot_general(a, b, (((0,), (0,)), ((), ())), preferred_element_type=F32)


def _layer_norm(y, g, b):
    mu = jnp.mean(y, axis=-1, keepdims=True)
    d = y - mu
    var = jnp.mean(d * d, axis=-1, keepdims=True)
    return d * lax.rsqrt(var + LN_EPS) * g + b


def _sigmoid(x):
    return 1.0 / (1.0 + jnp.exp(-x))


def _load_weights_bf16(pairs, stage, sem):
    chunks = [(src, dst, r) for src, dst in pairs for r in range(0, src.shape[0], LOAD_ROWS)]
    ahead = LOAD_SLOTS - 1

    def chunk_copy(i):
        src, _, r = chunks[i]
        slot = i % LOAD_SLOTS
        return pltpu.make_async_copy(src.at[pl.ds(r, LOAD_ROWS), :],
                                     stage.at[slot, :, pl.ds(0, src.shape[1])], sem.at[slot])

    for i in range(min(ahead, len(chunks))):
        chunk_copy(i).start()
    for i, (src, dst, r) in enumerate(chunks):
        if i + ahead < len(chunks):
            chunk_copy(i + ahead).start()
        chunk_copy(i).wait()
        dst[pl.ds(r, LOAD_ROWS), :] = stage[i % LOAD_SLOTS, :, 0:src.shape[1]].astype(BF16)


def _swiglu_ln(x, alpha, wg_ref, wu_ref, wd_ref, g_ref, b_ref, h_scr):
    xb = x.astype(BF16)
    d_ff = wg_ref.shape[1]
    for j in range(d_ff // FFN_COLS):
        sl = slice(j * FFN_COLS, (j + 1) * FFN_COLS)
        gate = _dot(xb, wg_ref[:, sl])
        up = _dot(xb, wu_ref[:, sl])
        h_scr[:, sl] = (gate * _sigmoid(gate) * up).astype(BF16)
    y = _dot(h_scr[...], wd_ref[...])
    return _layer_norm(alpha * x + 0.5 * y, g_ref[...], b_ref[...])


def _kv_kernel(mem_ref, g_ref, b_ref, wk_hbm, wv_hbm, k_ref, v_ref, wk_ref, wv_ref, stage, sem):
    @pl.when(pl.program_id(0) == 0)
    def _():
        _load_weights_bf16([(wk_hbm, wk_ref), (wv_hbm, wv_ref)], stage, sem)

    m = _layer_norm(mem_ref[...], g_ref[...], b_ref[...]).astype(BF16)
    k_ref[...] = _dot(m, wk_ref[...]).astype(BF16)
    v_ref[...] = _dot(m, wv_ref[...]).astype(BF16)


def _ffn_kernel(alpha, x_ref, wg_hbm, wu_hbm, wd_hbm, g_ref, b_ref, o_ref, h_scr,
                wg_ref, wu_ref, wd_ref, stage, sem):
    @pl.when(pl.program_id(0) == 0)
    def _():
        _load_weights_bf16([(wg_hbm, wg_ref), (wu_hbm, wu_ref), (wd_hbm, wd_ref)], stage, sem)

    for r in range(0, x_ref.shape[0], SUB_ROWS):
        rs = slice(r, r + SUB_ROWS)
        o_ref[rs, :] = _swiglu_ln(x_ref[rs, :], alpha, wg_ref, wu_ref, wd_ref, g_ref, b_ref, h_scr.at[rs, :])


def _bcast_row(a, blk, r):
    rows, w = a.shape
    a3 = a.reshape(rows // blk, blk, w)
    return jnp.broadcast_to(a3[:, r:r + 1, :], a3.shape).reshape(rows, w)


def _mixer_kernel(alpha, h_ref, w_in_hbm, lbl_ref, hgn_ref, sglg_ref, sglb_ref, ws_ref,
                  bst_ref, w_out_hbm, g_ref, b_ref, o_ref, st_ref, o_scr, mix_scr, sc_scr,
                  w_in_ref, w_out_ref, stage, sem):
    rows = h_ref.shape[0]
    W = HG_WIDTH

    @pl.when((pl.program_id(0) == 0) & (pl.program_id(1) == 0))
    def _():
        _load_weights_bf16([(w_in_hbm, w_in_ref), (w_out_hbm, w_out_ref)], stage, sem)

    @pl.when(pl.program_id(1) == 0)
    def _():
        st_ref[...] = jnp.zeros_like(st_ref)

    h = h_ref[...]
    proj = _dot(h.astype(BF16), w_in_ref[...])
    q = proj[:, 0:W]
    fz = proj[:, W:2 * W]
    iv = proj[:, 2 * W:3 * W]
    gt = proj[:, 3 * W:4 * W]
    uv = proj[:, 4 * W:]

    lbl = lbl_ref[...]
    lbe = jnp.exp(lbl - jnp.max(lbl, axis=0, keepdims=True))
    lb = lbe[0:1, :] / jnp.sum(lbe, axis=0, keepdims=True)

    e = jnp.exp(-jnp.abs(fz))
    r = 1.0 / (1.0 + e)
    er = e * r
    pos = fz >= 0.0
    f = lb + (1.0 - lb) * jnp.where(pos, r, er)
    k = (1.0 - lb) * jnp.where(pos, er, r)
    lf = jnp.log2(f)

    ri = lax.broadcasted_iota(jnp.int32, (CUMSUM_ROWS, CUMSUM_ROWS), 0)
    ci = lax.broadcasted_iota(jnp.int32, (CUMSUM_ROWS, CUMSUM_ROWS), 1)
    tri = ((ri >= ci) & (ri // HG_CHUNK == ci // HG_CHUNK)).astype(BF16)
    lf0 = lf.astype(BF16)
    rem = lf - lf0.astype(F32)
    lf1 = rem.astype(BF16)
    lf2 = (rem - lf1.astype(F32)).astype(BF16)
    b = jnp.concatenate(
        [_dot(tri, lf0[rs, :]) + _dot(tri, lf1[rs, :]) + _dot(tri, lf2[rs, :])
         for rs in (slice(i, i + CUMSUM_ROWS) for i in range(0, rows, CUMSUM_ROWS))], axis=0)

    rw = lax.broadcasted_iota(jnp.int32, (rows, W), 0)
    vb16 = iv.astype(BF16)
    n_chunks = rows // HG_CHUNK
    blocks = [(hh, c) for hh in range(HG_HEADS) for c in range(n_chunks)]
    col = lambda hh: slice(hh * HG_DK, (hh + 1) * HG_DK)
    row_ = lambda c: slice(c * HG_CHUNK, (c + 1) * HG_CHUNK)
    ti = lax.broadcasted_iota(jnp.int32, (HG_CHUNK, HG_CHUNK), 0)
    si = lax.broadcasted_iota(jnp.int32, (HG_CHUNK, HG_CHUNK), 1)

    b_end = _bcast_row(b, HG_CHUNK, HG_CHUNK - 1)
    q_st = (q * jnp.exp2(b)).astype(BF16)
    k_st = (k * jnp.exp2(b_end - b)).astype(BF16)
    dec_end = jnp.exp2(b_end)
    st_inc = {(hh, c): _dot_tn(vb16[row_(c), col(hh)], k_st[row_(c), col(hh)]) for hh, c in blocks}
    gate_act = gt * _sigmoid(gt)

    uvg = uv * (0.5 * (1.0 + jnp.tanh(0.7978845608028654 * (uv + 0.044715 * (uv * uv * uv)))))
    causal = (lax.broadcasted_iota(jnp.int32, (SG_CHUNK, SG_CHUNK), 0)
              >= lax.broadcasted_iota(jnp.int32, (SG_CHUNK, SG_CHUNK), 1))
    for g in range(SG_GROUPS):
        cs = slice(g * SG_DIM, (g + 1) * SG_DIM)
        u_g = uvg[:, cs]
        v_g = uvg[:, SG_WIDTH + g * SG_DIM:SG_WIDTH + (g + 1) * SG_DIM]
        vn = _layer_norm(v_g, sglg_ref[:, cs], sglb_ref[:, cs]).astype(BF16)
        w_g = jnp.where(causal, ws_ref[g], 0.0).astype(BF16)
        bias = bst_ref[:, g:g + 1]
        for n in range(rows // SG_CHUNK):
            rs = slice(n * SG_CHUNK, (n + 1) * SG_CHUNK)
            s_ = _dot(w_g, vn[rs, :]) + bias
            mix_scr[rs, HG_WIDTH + g * SG_DIM:HG_WIDTH + (g + 1) * SG_DIM] = (u_g[rs, :] * s_).astype(BF16)

    x = b - _bcast_row(b, HG_CHUNK, HG_CHUNK // 2 - 1)
    qm = (q * jnp.exp2(x)).astype(BF16)
    km = (k * jnp.exp2(-x)).astype(BF16)
    for i, (hh, c) in enumerate(blocks):
        s_ = _dot_nt(qm[row_(c), col(hh)], km[row_(c), col(hh)])
        sc_scr[i] = jnp.where(ti >= si, s_, 0.0).astype(BF16)
    chunk_ends = b.reshape(n_chunks, HG_CHUNK, W)[:, HG_CHUNK - 1, :]
    safe = jnp.max(-chunk_ends) < HG_SAFE_LOG2

    @pl.when(jnp.logical_not(safe))
    def _():
        lhs = [q.astype(BF16), jnp.where((rw % 2) == 1, q * f, k).astype(BF16)]
        for L in HG_LEVELS[1:]:
            if 2 * L >= 8:
                ref = _bcast_row(b, 2 * L, L - 1)
            else:
                ref = jnp.where((rw % 8) < 4, _bcast_row(b, 8, 1), _bcast_row(b, 8, 5))
            el = jnp.exp2(-jnp.abs(b - ref))
            upper = (rw % (2 * L)) >= L
            lhs.append((jnp.where(upper, q, k) * el).astype(BF16))
        rhs = [k.astype(BF16)] + lhs[1:]
        masks = [ti == si] + [((ti // (2 * L)) == (si // (2 * L))) & ((ti % (2 * L)) >= L) & ((si % (2 * L)) < L)
                              for L in HG_LEVELS]
        for i, (hh, c) in enumerate(blocks):
            s_ = jnp.zeros((HG_CHUNK, HG_CHUNK), F32)
            for a_, b_, m_ in zip(lhs, rhs, masks):
                s_ = jnp.where(m_, _dot_nt(a_[row_(c), col(hh)], b_[row_(c), col(hh)]), s_)
            sc_scr[i] = s_.astype(BF16)

    for hh in range(HG_HEADS):
        st = st_ref[hh]
        for c in range(n_chunks):
            o_scr[row_(c), col(hh)] = (_dot(sc_scr[hh * n_chunks + c], vb16[row_(c), col(hh)])
                                       + _dot_nt(q_st[row_(c), col(hh)], st.astype(BF16)))
            last = (c + 1) * HG_CHUNK - 1
            st = st * dec_end[last:last + 1, col(hh)] + st_inc[hh, c]
        st_ref[hh] = st

    o = o_scr[...]
    hgn = hgn_ref[...]
    for hh in range(HG_HEADS):
        cs = col(hh)
        oh = o[:, cs]
        ms = jnp.mean(oh * oh, axis=-1, keepdims=True)
        mix_scr[:, cs] = (oh * lax.rsqrt(ms + LN_EPS) * hgn * gate_act[:, cs]).astype(BF16)

    for r in range(0, rows, OUT_ROWS):
        rs = slice(r, r + OUT_ROWS)
        mix = _dot(mix_scr[rs, :], w_out_ref[...])
        o_ref[rs, :] = _layer_norm(alpha * h[rs, :] + mix, g_ref[...], b_ref[...])


def _xattn_ffn_kernel(alpha, h_ref, k_ref, v_ref, wq_hbm, wo_hbm, g3_ref, b3_ref,
                      wg_hbm, wu_hbm, wd_hbm, g4_ref, b4_ref, o_ref, a_scr, h_scr,
                      wq_ref, wo_ref, wg_ref, wu_ref, wd_ref, stage, sem):
    @pl.when(pl.program_id(0) == 0)
    def _():
        _load_weights_bf16([(wq_hbm, wq_ref), (wo_hbm, wo_ref), (wg_hbm, wg_ref), (wu_hbm, wu_ref),
                            (wd_hbm, wd_ref)], stage, sem)

    h = h_ref[...]
    d_model = h.shape[1]
    hd = d_model // X_HEADS
    qb = (_dot(h.astype(BF16), wq_ref[...]) * (hd ** -0.5)).astype(BF16)
    for hh in range(X_HEADS):
        cs = slice(hh * hd, (hh + 1) * hd)
        s = _dot_nt(qb[:, cs], k_ref[:, cs])
        p = jnp.exp(s - jnp.max(s, axis=-1, keepdims=True))
        l = jnp.sum(p, axis=-1, keepdims=True)
        a_scr[:, cs] = (_dot(p.astype(BF16), v_ref[:, cs]) / l).astype(BF16)
    groups = [slice(r, r + SUB_ROWS) for r in range(0, h.shape[0], SUB_ROWS)]
    h3 = [_layer_norm(alpha * h[rs, :] + _dot(a_scr[rs, :], wo_ref[...]), g3_ref[...], b3_ref[...])
          for rs in groups]
    for rs, h3_g in zip(groups, h3):
        o_ref[rs, :] = _swiglu_ln(h3_g, alpha, wg_ref, wu_ref, wd_ref, g4_ref, b4_ref, h_scr.at[rs, :])


def _resident(shape):
    return pl.BlockSpec(shape, lambda *_: (0,) * len(shape), pipeline_mode=pl.Buffered(1))


_HBM = pl.BlockSpec(memory_space=pl.ANY)


def _weight_scratch(shapes):
    width = max(c for _, c in shapes)
    return ([pltpu.VMEM(shape, BF16) for shape in shapes]
            + [pltpu.VMEM((LOAD_SLOTS, LOAD_ROWS, width), F32), pltpu.SemaphoreType.DMA((LOAD_SLOTS,))])


def _params(n_grid_axes):
    return pltpu.CompilerParams(dimension_semantics=("arbitrary",) * n_grid_axes,
                                vmem_limit_bytes=VMEM_LIMIT_BYTES)


def kernel(x, mem, ffn1_w_gate, ffn1_w_up, ffn1_w_down, ln1_g, ln1_b, w_in, hg_lb_logits, hg_norm_g, sg_ln_g, sg_ln_b, sg_w_s, sg_b_s, w_out, ln2_g, ln2_b, mem_ln_g, mem_ln_b, xa_w_q, xa_w_k, xa_w_v, xa_w_o, ln3_g, ln3_b, ffn2_w_gate, ffn2_w_up, ffn2_w_down, ln4_g, ln4_b):
    depth = w_in.shape[0]
    assert depth == 1, "single-layer stack"
    B, T, D = x.shape
    n_rows = B * T
    d_ff = ffn1_w_gate.shape[2]
    alpha = (2.0 * depth) ** 0.25
    assert T % MIX_ROWS == 0 and T % FFN_ROWS == 0 and T % FFN1_ROWS == 0 and d_ff % FFN_COLS == 0
    assert MIX_ROWS % SG_CHUNK == 0 and MIX_ROWS % HG_CHUNK == 0
    assert D % LOAD_ROWS == 0 and d_ff % LOAD_ROWS == 0

    row = lambda a: a.reshape(1, -1)
    mat = lambda a: a.reshape(a.shape[1:])
    x2 = x.reshape(n_rows, D)
    mem2 = mem.reshape(B * MEM_LEN, D)

    k_mem, v_mem = pl.pallas_call(
        _kv_kernel,
        grid=(B,),
        in_specs=[pl.BlockSpec((MEM_LEN, D), lambda i: (i, 0)),
                  _resident((1, D)), _resident((1, D)), _HBM, _HBM],
        out_specs=[pl.BlockSpec((MEM_LEN, D), lambda i: (i, 0))] * 2,
        out_shape=[jax.ShapeDtypeStruct((B * MEM_LEN, D), BF16)] * 2,
        scratch_shapes=_weight_scratch([(D, D), (D, D)]),
        compiler_params=_params(1),
        name="memory_kv",
    )(mem2, row(mem_ln_g[0]), row(mem_ln_b[0]), mat(xa_w_k), mat(xa_w_v))

    row1_spec = pl.BlockSpec((FFN1_ROWS, D), lambda i: (i, 0))
    h1 = pl.pallas_call(
        functools.partial(_ffn_kernel, alpha),
        grid=(n_rows // FFN1_ROWS,),
        in_specs=[row1_spec, _HBM, _HBM, _HBM, _resident((1, D)), _resident((1, D))],
        out_specs=row1_spec,
        out_shape=jax.ShapeDtypeStruct((n_rows, D), F32),
        scratch_shapes=[pltpu.VMEM((FFN1_ROWS, d_ff), BF16)]
        + _weight_scratch([(D, d_ff), (D, d_ff), (d_ff, D)]),
        compiler_params=_params(1),
        name="ffn1_ln1",
    )(x2, mat(ffn1_w_gate), mat(ffn1_w_up), mat(ffn1_w_down), row(ln1_g[0]), row(ln1_b[0]))

    tiles_per_seq = T // MIX_ROWS
    in_width = w_in.shape[2]
    mix_spec = pl.BlockSpec((MIX_ROWS, D), lambda bi, ti: (bi * tiles_per_seq + ti, 0))
    h2 = pl.pallas_call(
        functools.partial(_mixer_kernel, alpha),
        grid=(B, tiles_per_seq),
        in_specs=[mix_spec, _HBM,
                  _resident((hg_lb_logits.shape[0], HG_WIDTH)), _resident((1, HG_DK)),
                  _resident((1, SG_WIDTH)), _resident((1, SG_WIDTH)),
                  _resident((SG_GROUPS, SG_CHUNK, SG_CHUNK)), _resident((SG_CHUNK, SG_GROUPS)),
                  _HBM, _resident((1, D)), _resident((1, D))],
        out_specs=mix_spec,
        out_shape=jax.ShapeDtypeStruct((n_rows, D), F32),
        scratch_shapes=[pltpu.VMEM((HG_HEADS, HG_DK, HG_DK), F32),
                        pltpu.VMEM((MIX_ROWS, HG_WIDTH), F32),
                        pltpu.VMEM((MIX_ROWS, D), BF16),
                        pltpu.VMEM((HG_HEADS * (MIX_ROWS // HG_CHUNK), HG_CHUNK, HG_CHUNK), BF16)]
        + _weight_scratch([(D, in_width), (D, D)]),
        compiler_params=_params(2),
        name="mixer_ln2",
    )(h1, mat(w_in), hg_lb_logits.reshape(hg_lb_logits.shape[0], HG_WIDTH), row(hg_norm_g[0]),
      row(sg_ln_g[0]), row(sg_ln_b[0]), sg_w_s[0], sg_b_s[0].T, mat(w_out),
      row(ln2_g[0]), row(ln2_b[0]))

    row_spec = pl.BlockSpec((FFN_ROWS, D), lambda i: (i, 0))
    tiles_per_batch = T // FFN_ROWS
    kv_spec = pl.BlockSpec((MEM_LEN, D), lambda i: (i // tiles_per_batch, 0))
    out = pl.pallas_call(
        functools.partial(_xattn_ffn_kernel, alpha),
        grid=(n_rows // FFN_ROWS,),
        in_specs=[row_spec, kv_spec, kv_spec, _HBM, _HBM, _resident((1, D)), _resident((1, D)),
                  _HBM, _HBM, _HBM, _resident((1, D)), _resident((1, D))],
        out_specs=row_spec,
        out_shape=jax.ShapeDtypeStruct((n_rows, D), F32),
        scratch_shapes=[pltpu.VMEM((FFN_ROWS, D), BF16), pltpu.VMEM((FFN_ROWS, d_ff), BF16)]
        + _weight_scratch([(D, D), (D, D), (D, d_ff), (D, d_ff), (d_ff, D)]),
        compiler_params=_params(1),
        name="xattn_ln3_ffn2_ln4",
    )(h2, k_mem, v_mem, mat(xa_w_q), mat(xa_w_o), row(ln3_g[0]), row(ln3_b[0]),
      mat(ffn2_w_gate), mat(ffn2_w_up), mat(ffn2_w_down), row(ln4_g[0]), row(ln4_b[0]))

    return out.reshape(B, T, D)
```

```python
import functools

import jax
import jax.numpy as jnp
from jax import lax
from jax.experimental import pallas as pl
from jax.experimental.pallas import tpu as pltpu

F32 = jnp.float32
BF16 = jnp.bfloat16

LN_EPS = 1e-5
MEM_LEN = 256
HG_HEADS = 4
HG_DK = 128
HG_WIDTH = HG_HEADS * HG_DK
SG_GROUPS = 4
SG_DIM = 128
SG_WIDTH = SG_GROUPS * SG_DIM
SG_CHUNK = 128
X_HEADS = 4

HG_CHUNK = 64
HG_LEVELS = (1, 2, 4, 8, 16, 32)
HG_SAFE_LOG2 = 100.0

FFN1_ROWS = 1024
FFN_ROWS = 512
SUB_ROWS = 256
FFN_COLS = 256
MIX_ROWS = 512
CUMSUM_ROWS = 256
OUT_ROWS = 128
LOAD_ROWS = 128
LOAD_SLOTS = 4
VMEM_LIMIT_BYTES = 56 * 1024 * 1024


def _dot(a, b):
    return jnp.dot(a, b, preferred_element_type=F32)


def _dot_nt(a, b):
    return lax.dot_general(a, b, (((1,), (1,)), ((), ())), preferred_element_type=F32)


def _dot_tn(a, b):
    return lax.dot_general(a, b, (((0,), (0,)), ((), ())), preferred_element_type=F32)


def _layer_norm(y, g, b):
    mu = jnp.mean(y, axis=-1, keepdims=True)
    d = y - mu
    var = jnp.mean(d * d, axis=-1, keepdims=True)
    return d * lax.rsqrt(var + LN_EPS) * g + b


def _sigmoid(x):
    return 1.0 / (1.0 + jnp.exp(-x))


def _load_weights_bf16(pairs, stage, sem):
    chunks = [(src, dst, r) for src, dst in pairs for r in range(0, src.shape[0], LOAD_ROWS)]
    ahead = LOAD_SLOTS - 1

    def chunk_copy(i):
        src, _, r = chunks[i]
        slot = i % LOAD_SLOTS
        return pltpu.make_async_copy(src.at[pl.ds(r, LOAD_ROWS), :],
                                     stage.at[slot, :, pl.ds(0, src.shape[1])], sem.at[slot])

    for i in range(min(ahead, len(chunks))):
        chunk_copy(i).start()
    for i, (src, dst, r) in enumerate(chunks):
        if i + ahead < len(chunks):
            chunk_copy(i + ahead).start()
        chunk_copy(i).wait()
        dst[pl.ds(r, LOAD_ROWS), :] = stage[i % LOAD_SLOTS, :, 0:src.shape[1]].astype(BF16)


def _after(x, deps):
    always = None
    for d in deps:
        c = (d == d) | (d != d)
        always = c if always is None else always & c
    t = jnp.where(always, -jnp.inf, deps[0]).astype(x.dtype)
    return jnp.maximum(x, jnp.tile(t, (x.shape[0] // 16, x.shape[1] // 128)))


def _swiglu_ln(x, alpha, wg_ref, wu_ref, wd_ref, g_ref, b_ref, h_scr):
    xb = x.astype(BF16)
    d_ff = wg_ref.shape[1]
    for j in range(d_ff // FFN_COLS):
        sl = slice(j * FFN_COLS, (j + 1) * FFN_COLS)
        gate = _dot(xb, wg_ref[:, sl])
        up = _dot(xb, wu_ref[:, sl])
        h_scr[:, sl] = (gate * _sigmoid(gate) * up).astype(BF16)
    y = _dot(h_scr[...], wd_ref[...])
    return _layer_norm(alpha * x + 0.5 * y, g_ref[...], b_ref[...])


def _kv_kernel(mem_ref, g_ref, b_ref, wk_hbm, wv_hbm, k_ref, v_ref, wk_ref, wv_ref, stage, sem):
    @pl.when(pl.program_id(0) == 0)
    def _():
        _load_weights_bf16([(wk_hbm, wk_ref), (wv_hbm, wv_ref)], stage, sem)

    m = _layer_norm(mem_ref[...], g_ref[...], b_ref[...]).astype(BF16)
    k_ref[...] = _dot(m, wk_ref[...]).astype(BF16)
    v_ref[...] = _dot(m, wv_ref[...]).astype(BF16)


def _ffn_kernel(alpha, x_ref, wg_hbm, wu_hbm, wd_hbm, g_ref, b_ref, o_ref, h_scr,
                wg_ref, wu_ref, wd_ref, stage, sem):
    @pl.when(pl.program_id(0) == 0)
    def _():
        _load_weights_bf16([(wg_hbm, wg_ref), (wu_hbm, wu_ref), (wd_hbm, wd_ref)], stage, sem)

    for r in range(0, x_ref.shape[0], SUB_ROWS):
        rs = slice(r, r + SUB_ROWS)
        o_ref[rs, :] = _swiglu_ln(x_ref[rs, :], alpha, wg_ref, wu_ref, wd_ref, g_ref, b_ref, h_scr.at[rs, :])


def _bcast_row(a, blk, r):
    rows, w = a.shape
    a3 = a.reshape(rows // blk, blk, w)
    return jnp.broadcast_to(a3[:, r:r + 1, :], a3.shape).reshape(rows, w)


def _mixer_kernel(alpha, h_ref, w_in_hbm, lbl_ref, hgn_ref, sglg_ref, sglb_ref, ws_ref,
                  bst_ref, w_out_hbm, g_ref, b_ref, o_ref, st_ref, o_scr, mix_scr, sc_scr,
                  w_in_ref, w_out_ref, stage, sem):
    rows = h_ref.shape[0]
    W = HG_WIDTH

    @pl.when((pl.program_id(0) == 0) & (pl.program_id(1) == 0))
    def _():
        _load_weights_bf16([(w_in_hbm, w_in_ref), (w_out_hbm, w_out_ref)], stage, sem)

    @pl.when(pl.program_id(1) == 0)
    def _():
        st_ref[...] = jnp.zeros_like(st_ref)

    h = h_ref[...]
    hb = h.astype(BF16)
    fz = _dot(hb, w_in_ref[:, W:2 * W])
    uv = _dot(hb, w_in_ref[:, 4 * W:])

    lbl = lbl_ref[...]
    lbe = jnp.exp(lbl - jnp.max(lbl, axis=0, keepdims=True))
    lb = lbe[0:1, :] / jnp.sum(lbe, axis=0, keepdims=True)

    e = jnp.exp(-jnp.abs(fz))
    r = 1.0 / (1.0 + e)
    er = e * r
    pos = fz >= 0.0
    f = lb + (1.0 - lb) * jnp.where(pos, r, er)
    k = (1.0 - lb) * jnp.where(pos, er, r)
    lf = jnp.log2(f)

    ri = lax.broadcasted_iota(jnp.int32, (CUMSUM_ROWS, CUMSUM_ROWS), 0)
    ci = lax.broadcasted_iota(jnp.int32, (CUMSUM_ROWS, CUMSUM_ROWS), 1)
    tri = ((ri >= ci) & (ri // HG_CHUNK == ci // HG_CHUNK)).astype(BF16)
    lf0 = lf.astype(BF16)
    rem = lf - lf0.astype(F32)
    lf1 = rem.astype(BF16)
    lf2 = (rem - lf1.astype(F32)).astype(BF16)
    b = jnp.concatenate(
        [_dot(tri, lf0[rs, :]) + _dot(tri, lf1[rs, :]) + _dot(tri, lf2[rs, :])
         for rs in (slice(i, i + CUMSUM_ROWS) for i in range(0, rows, CUMSUM_ROWS))], axis=0)

    q = _dot(_after(hb, [b[i - 16:i, 0:HG_DK] for i in range(CUMSUM_ROWS, rows + 1, CUMSUM_ROWS)]),
             w_in_ref[:, 0:W])
    rw = lax.broadcasted_iota(jnp.int32, (rows, W), 0)
    n_chunks = rows // HG_CHUNK
    blocks = [(hh, c) for hh in range(HG_HEADS) for c in range(n_chunks)]
    col = lambda hh: slice(hh * HG_DK, (hh + 1) * HG_DK)
    row_ = lambda c: slice(c * HG_CHUNK, (c + 1) * HG_CHUNK)
    ti = lax.broadcasted_iota(jnp.int32, (HG_CHUNK, HG_CHUNK), 0)
    si = lax.broadcasted_iota(jnp.int32, (HG_CHUNK, HG_CHUNK), 1)

    uvg = uv * (0.5 * (1.0 + jnp.tanh(0.7978845608028654 * (uv + 0.044715 * (uv * uv * uv)))))
    causal = (lax.broadcasted_iota(jnp.int32, (SG_CHUNK, SG_CHUNK), 0)
              >= lax.broadcasted_iota(jnp.int32, (SG_CHUNK, SG_CHUNK), 1))
    sg_tail = []
    for g in range(SG_GROUPS):
        cs = slice(g * SG_DIM, (g + 1) * SG_DIM)
        u_g = uvg[:, cs]
        v_g = uvg[:, SG_WIDTH + g * SG_DIM:SG_WIDTH + (g + 1) * SG_DIM]
        vn = _layer_norm(v_g, sglg_ref[:, cs], sglb_ref[:, cs]).astype(BF16)
        w_g = jnp.where(causal, ws_ref[g], 0.0).astype(BF16)
        bias = bst_ref[:, g:g + 1]
        for n in range(rows // SG_CHUNK):
            rs = slice(n * SG_CHUNK, (n + 1) * SG_CHUNK)
            s_ = _dot(w_g, vn[rs, :]) + bias
            mix_scr[rs, HG_WIDTH + g * SG_DIM:HG_WIDTH + (g + 1) * SG_DIM] = (u_g[rs, :] * s_).astype(BF16)
        sg_tail.append(s_[0:16, :])
    iv = _dot(_after(hb, sg_tail[:SG_GROUPS // 2]), w_in_ref[:, 2 * W:3 * W])
    gt = _dot(_after(hb, sg_tail[SG_GROUPS // 2:]), w_in_ref[:, 3 * W:4 * W])
    vb16 = iv.astype(BF16)

    b_end = _bcast_row(b, HG_CHUNK, HG_CHUNK - 1)
    q_st = (q * jnp.exp2(b)).astype(BF16)
    k_st = (k * jnp.exp2(b_end - b)).astype(BF16)
    dec_end = jnp.exp2(b_end)
    st_inc = {(hh, c): _dot_tn(vb16[row_(c), col(hh)], k_st[row_(c), col(hh)]) for hh, c in blocks}
    gate_act = gt * _sigmoid(gt)

    x = b - _bcast_row(b, HG_CHUNK, HG_CHUNK // 2 - 1)
    qm = (q * jnp.exp2(x)).astype(BF16)
    km = (k * jnp.exp2(-x)).astype(BF16)
    for i, (hh, c) in enumerate(blocks):
        s_ = _dot_nt(qm[row_(c), col(hh)], km[row_(c), col(hh)])
        sc_scr[i] = jnp.where(ti >= si, s_, 0.0).astype(BF16)
    chunk_ends = b.reshape(n_chunks, HG_CHUNK, W)[:, HG_CHUNK - 1, :]
    safe = jnp.max(-chunk_ends) < HG_SAFE_LOG2

    @pl.when(jnp.logical_not(safe))
    def _():
        lhs = [q.astype(BF16), jnp.where((rw % 2) == 1, q * f, k).astype(BF16)]
        for L in HG_LEVELS[1:]:
            if 2 * L >= 8:
                ref = _bcast_row(b, 2 * L, L - 1)
            else:
                ref = jnp.where((rw % 8) < 4, _bcast_row(b, 8, 1), _bcast_row(b, 8, 5))
            el = jnp.exp2(-jnp.abs(b - ref))
            upper = (rw % (2 * L)) >= L
            lhs.append((jnp.where(upper, q, k) * el).astype(BF16))
        rhs = [k.astype(BF16)] + lhs[1:]
        masks = [ti == si] + [((ti // (2 * L)) == (si // (2 * L))) & ((ti % (2 * L)) >= L) & ((si % (2 * L)) < L)
                              for L in HG_LEVELS]
        for i, (hh, c) in enumerate(blocks):
            s_ = jnp.zeros((HG_CHUNK, HG_CHUNK), F32)
            for a_, b_, m_ in zip(lhs, rhs, masks):
                s_ = jnp.where(m_, _dot_nt(a_[row_(c), col(hh)], b_[row_(c), col(hh)]), s_)
            sc_scr[i] = s_.astype(BF16)

    for hh in range(HG_HEADS):
        st = st_ref[hh]
        for c in range(n_chunks):
            o_scr[row_(c), col(hh)] = (_dot(sc_scr[hh * n_chunks + c], vb16[row_(c), col(hh)])
                                       + _dot_nt(q_st[row_(c), col(hh)], st.astype(BF16)))
            last = (c + 1) * HG_CHUNK - 1
            st = st * dec_end[last:last + 1, col(hh)] + st_inc[hh, c]
        st_ref[hh] = st

    o = o_scr[...]
    hgn = hgn_ref[...]
    for hh in range(HG_HEADS):
        cs = col(hh)
        oh = o[:, cs]
        ms = jnp.mean(oh * oh, axis=-1, keepdims=True)
        mix_scr[:, cs] = (oh * lax.rsqrt(ms + LN_EPS) * hgn * gate_act[:, cs]).astype(BF16)

    for r in range(0, rows, OUT_ROWS):
        rs = slice(r, r + OUT_ROWS)
        mix = _dot(mix_scr[rs, :], w_out_ref[...])
        o_ref[rs, :] = _layer_norm(alpha * h[rs, :] + mix, g_ref[...], b_ref[...])


def _xattn_ffn_kernel(alpha, h_ref, k_ref, v_ref, wq_hbm, wo_hbm, g3_ref, b3_ref,
                      wg_hbm, wu_hbm, wd_hbm, g4_ref, b4_ref, o_ref, a_scr, h_scr,
                      wq_ref, wo_ref, wg_ref, wu_ref, wd_ref, stage, sem):
    @pl.when(pl.program_id(0) == 0)
    def _():
        _load_weights_bf16([(wq_hbm, wq_ref), (wo_hbm, wo_ref), (wg_hbm, wg_ref), (wu_hbm, wu_ref),
                            (wd_hbm, wd_ref)], stage, sem)

    h = h_ref[...]
    d_model = h.shape[1]
    hd = d_model // X_HEADS
    qb = (_dot(h.astype(BF16), wq_ref[...]) * (hd ** -0.5)).astype(BF16)
    for hh in range(X_HEADS):
        cs = slice(hh * hd, (hh + 1) * hd)
        s = _dot_nt(qb[:, cs], k_ref[:, cs])
        p = jnp.exp(s - jnp.max(s, axis=-1, keepdims=True))
        l = jnp.sum(p, axis=-1, keepdims=True)
        a_scr[:, cs] = (_dot(p.astype(BF16), v_ref[:, cs]) / l).astype(BF16)
    groups = [slice(r, r + SUB_ROWS) for r in range(0, h.shape[0], SUB_ROWS)]
    h3 = [_layer_norm(alpha * h[rs, :] + _dot(a_scr[rs, :], wo_ref[...]), g3_ref[...], b3_ref[...])
          for rs in groups]
    for rs, h3_g in zip(groups, h3):
        o_ref[rs, :] = _swiglu_ln(h3_g, alpha, wg_ref, wu_ref, wd_ref, g4_ref, b4_ref, h_scr.at[rs, :])


def _resident(shape):
    return pl.BlockSpec(shape, lambda *_: (0,) * len(shape), pipeline_mode=pl.Buffered(1))


_HBM = pl.BlockSpec(memory_space=pl.ANY)


def _weight_scratch(shapes):
    width = max(c for _, c in shapes)
    return ([pltpu.VMEM(shape, BF16) for shape in shapes]
            + [pltpu.VMEM((LOAD_SLOTS, LOAD_ROWS, width), F32), pltpu.SemaphoreType.DMA((LOAD_SLOTS,))])


def _params(n_grid_axes):
    return pltpu.CompilerParams(dimension_semantics=("arbitrary",) * n_grid_axes,
                                vmem_limit_bytes=VMEM_LIMIT_BYTES)


def kernel(x, mem, ffn1_w_gate, ffn1_w_up, ffn1_w_down, ln1_g, ln1_b, w_in, hg_lb_logits, hg_norm_g, sg_ln_g, sg_ln_b, sg_w_s, sg_b_s, w_out, ln2_g, ln2_b, mem_ln_g, mem_ln_b, xa_w_q, xa_w_k, xa_w_v, xa_w_o, ln3_g, ln3_b, ffn2_w_gate, ffn2_w_up, ffn2_w_down, ln4_g, ln4_b):
    depth = w_in.shape[0]
    assert depth == 1, "single-layer stack"
    B, T, D = x.shape
    n_rows = B * T
    d_ff = ffn1_w_gate.shape[2]
    alpha = (2.0 * depth) ** 0.25
    assert T % MIX_ROWS == 0 and T % FFN_ROWS == 0 and T % FFN1_ROWS == 0 and d_ff % FFN_COLS == 0
    assert MIX_ROWS % SG_CHUNK == 0 and MIX_ROWS % HG_CHUNK == 0
    assert D % LOAD_ROWS == 0 and d_ff % LOAD_ROWS == 0

    row = lambda a: a.reshape(1, -1)
    mat = lambda a: a.reshape(a.shape[1:])
    x2 = x.reshape(n_rows, D)
    mem2 = mem.reshape(B * MEM_LEN, D)

    k_mem, v_mem = pl.pallas_call(
        _kv_kernel,
        grid=(B,),
        in_specs=[pl.BlockSpec((MEM_LEN, D), lambda i: (i, 0)),
                  _resident((1, D)), _resident((1, D)), _HBM, _HBM],
        out_specs=[pl.BlockSpec((MEM_LEN, D), lambda i: (i, 0))] * 2,
        out_shape=[jax.ShapeDtypeStruct((B * MEM_LEN, D), BF16)] * 2,
        scratch_shapes=_weight_scratch([(D, D), (D, D)]),
        compiler_params=_params(1),
        name="memory_kv",
    )(mem2, row(mem_ln_g[0]), row(mem_ln_b[0]), mat(xa_w_k), mat(xa_w_v))

    row1_spec = pl.BlockSpec((FFN1_ROWS, D), lambda i: (i, 0))
    h1 = pl.pallas_call(
        functools.partial(_ffn_kernel, alpha),
        grid=(n_rows // FFN1_ROWS,),
        in_specs=[row1_spec, _HBM, _HBM, _HBM, _resident((1, D)), _resident((1, D))],
        out_specs=row1_spec,
        out_shape=jax.ShapeDtypeStruct((n_rows, D), F32),
        scratch_shapes=[pltpu.VMEM((FFN1_ROWS, d_ff), BF16)]
        + _weight_scratch([(D, d_ff), (D, d_ff), (d_ff, D)]),
        compiler_params=_params(1),
        name="ffn1_ln1",
    )(x2, mat(ffn1_w_gate), mat(ffn1_w_up), mat(ffn1_w_down), row(ln1_g[0]), row(ln1_b[0]))

    tiles_per_seq = T // MIX_ROWS
    in_width = w_in.shape[2]
    mix_spec = pl.BlockSpec((MIX_ROWS, D), lambda bi, ti: (bi * tiles_per_seq + ti, 0))
    h2 = pl.pallas_call(
        functools.partial(_mixer_kernel, alpha),
        grid=(B, tiles_per_seq),
        in_specs=[mix_spec, _HBM,
                  _resident((hg_lb_logits.shape[0], HG_WIDTH)), _resident((1, HG_DK)),
                  _resident((1, SG_WIDTH)), _resident((1, SG_WIDTH)),
                  _resident((SG_GROUPS, SG_CHUNK, SG_CHUNK)), _resident((SG_CHUNK, SG_GROUPS)),
                  _HBM, _resident((1, D)), _resident((1, D))],
        out_specs=mix_spec,
        out_shape=jax.ShapeDtypeStruct((n_rows, D), F32),
        scratch_shapes=[pltpu.VMEM((HG_HEADS, HG_DK, HG_DK), F32),
                        pltpu.VMEM((MIX_ROWS, HG_WIDTH), F32),
                        pltpu.VMEM((MIX_ROWS, D), BF16),
                        pltpu.VMEM((HG_HEADS * (MIX_ROWS // HG_CHUNK), HG_CHUNK, HG_CHUNK), BF16)]
        + _weight_scratch([(D, in_width), (D, D)]),
        compiler_params=_params(2),
        name="mixer_ln2",
    )(h1, mat(w_in), hg_lb_logits.reshape(hg_lb_logits.shape[0], HG_WIDTH), row(hg_norm_g[0]),
      row(sg_ln_g[0]), row(sg_ln_b[0]), sg_w_s[0], sg_b_s[0].T, mat(w_out),
      row(ln2_g[0]), row(ln2_b[0]))

    row_spec = pl.BlockSpec((FFN_ROWS, D), lambda i: (i, 0))
    tiles_per_batch = T // FFN_ROWS
    kv_spec = pl.BlockSpec((MEM_LEN, D), lambda i: (i // tiles_per_batch, 0))
    out = pl.pallas_call(
        functools.partial(_xattn_ffn_kernel, alpha),
        grid=(n_rows // FFN_ROWS,),
        in_specs=[row_spec, kv_spec, kv_spec, _HBM, _HBM, _resident((1, D)), _resident((1, D)),
                  _HBM, _HBM, _HBM, _resident((1, D)), _resident((1, D))],
        out_specs=row_spec,
        out_shape=jax.ShapeDtypeStruct((n_rows, D), F32),
        scratch_shapes=[pltpu.VMEM((FFN_ROWS, D), BF16), pltpu.VMEM((FFN_ROWS, d_ff), BF16)]
        + _weight_scratch([(D, D), (D, D), (D, d_ff), (D, d_ff), (d_ff, D)]),
        compiler_params=_params(1),
        name="xattn_ln3_ffn2_ln4",
    )(h2, k_mem, v_mem, mat(xa_w_q), mat(xa_w_o), row(ln3_g[0]), row(ln3_b[0]),
      mat(ffn2_w_gate), mat(ffn2_w_up), mat(ffn2_w_down), row(ln4_g[0]), row(ln4_b[0]))

    return out.reshape(B, T, D)
```

```python
import functools

import jax
import jax.numpy as jnp
from jax import lax
from jax.experimental import pallas as pl
from jax.experimental.pallas import tpu as pltpu

F32 = jnp.float32
BF16 = jnp.bfloat16

LN_EPS = 1e-5
MEM_LEN = 256
HG_HEADS = 4
HG_DK = 128
HG_WIDTH = HG_HEADS * HG_DK
SG_GROUPS = 4
SG_DIM = 128
SG_WIDTH = SG_GROUPS * SG_DIM
SG_CHUNK = 128
X_HEADS = 4

HG_CHUNK = 64
HG_LEVELS = (1, 2, 4, 8, 16, 32)
HG_SAFE_LOG2 = 100.0

FFN1_ROWS = 1024
FFN_ROWS = 1024
SUB_ROWS = 256
FFN_COLS = 256
MIX_ROWS = 512
CUMSUM_ROWS = 256
OUT_ROWS = 128
LOAD_ROWS = 128
LOAD_SLOTS = 4
VMEM_LIMIT_BYTES = 56 * 1024 * 1024


def _dot(a, b):
    return jnp.dot(a, b, preferred_element_type=F32)


def _dot_nt(a, b):
    return lax.dot_general(a, b, (((1,), (1,)), ((), ())), preferred_element_type=F32)


def _dot_tn(a, b):
    return lax.dot_general(a, b, (((0,), (0,)), ((), ())), preferred_element_type=F32)


def _layer_norm(y, g, b):
    mu = jnp.mean(y, axis=-1, keepdims=True)
    d = y - mu
    var = jnp.mean(d * d, axis=-1, keepdims=True)
    return d * lax.rsqrt(var + LN_EPS) * g + b


def _sigmoid(x):
    return 1.0 / (1.0 + jnp.exp(-x))


def _load_weights_bf16(pairs, stage, sem):
    chunks = [(src, dst, r) for src, dst in pairs for r in range(0, src.shape[0], LOAD_ROWS)]
    ahead = LOAD_SLOTS - 1

    def chunk_copy(i):
        src, _, r = chunks[i]
        slot = i % LOAD_SLOTS
        return pltpu.make_async_copy(src.at[pl.ds(r, LOAD_ROWS), :],
                                     stage.at[slot, :, pl.ds(0, src.shape[1])], sem.at[slot])

    for i in range(min(ahead, len(chunks))):
        chunk_copy(i).start()
    for i, (src, dst, r) in enumerate(chunks):
        if i + ahead < len(chunks):
            chunk_copy(i + ahead).start()
        chunk_copy(i).wait()
        dst[pl.ds(r, LOAD_ROWS), :] = stage[i % LOAD_SLOTS, :, 0:src.shape[1]].astype(BF16)


def _after(x, deps):
    always = None
    for d in deps:
        c = (d == d) | (d != d)
        always = c if always is None else always & c
    t = jnp.where(always, -jnp.inf, deps[0]).astype(x.dtype)
    return jnp.maximum(x, jnp.tile(t, (x.shape[0] // 16, x.shape[1] // 128)))


def _swiglu_ln(x, alpha, wg_ref, wu_ref, wd_ref, g_ref, b_ref, h_scr):
    xb = x.astype(BF16)
    d_ff = wg_ref.shape[1]
    for j in range(d_ff // FFN_COLS):
        sl = slice(j * FFN_COLS, (j + 1) * FFN_COLS)
        gate = _dot(xb, wg_ref[:, sl])
        up = _dot(xb, wu_ref[:, sl])
        h_scr[:, sl] = (gate * _sigmoid(gate) * up).astype(BF16)
    y = _dot(h_scr[...], wd_ref[...])
    return _layer_norm(alpha * x + 0.5 * y, g_ref[...], b_ref[...])


def _kv_kernel(mem_ref, g_ref, b_ref, wk_hbm, wv_hbm, k_ref, v_ref, wk_ref, wv_ref, stage, sem):
    @pl.when(pl.program_id(0) == 0)
    def _():
        _load_weights_bf16([(wk_hbm, wk_ref), (wv_hbm, wv_ref)], stage, sem)

    m = _layer_norm(mem_ref[...], g_ref[...], b_ref[...]).astype(BF16)
    k_ref[...] = _dot(m, wk_ref[...]).astype(BF16)
    v_ref[...] = _dot(m, wv_ref[...]).astype(BF16)


def _ffn_kernel(alpha, x_ref, wg_hbm, wu_hbm, wd_hbm, g_ref, b_ref, o_ref, h_scr,
                wg_ref, wu_ref, wd_ref, stage, sem):
    @pl.when(pl.program_id(0) == 0)
    def _():
        _load_weights_bf16([(wg_hbm, wg_ref), (wu_hbm, wu_ref), (wd_hbm, wd_ref)], stage, sem)

    for r in range(0, x_ref.shape[0], SUB_ROWS):
        rs = slice(r, r + SUB_ROWS)
        o_ref[rs, :] = _swiglu_ln(x_ref[rs, :], alpha, wg_ref, wu_ref, wd_ref, g_ref, b_ref, h_scr.at[rs, :])


def _bcast_row(a, blk, r):
    rows, w = a.shape
    a3 = a.reshape(rows // blk, blk, w)
    return jnp.broadcast_to(a3[:, r:r + 1, :], a3.shape).reshape(rows, w)


def _mixer_kernel(alpha, h_ref, w_in_hbm, lbl_ref, hgn_ref, sglg_ref, sglb_ref, ws_ref,
                  bst_ref, w_out_hbm, g_ref, b_ref, o_ref, st_ref, o_scr, mix_scr, sc_scr,
                  w_in_ref, w_out_ref, stage, sem):
    rows = h_ref.shape[0]
    W = HG_WIDTH

    @pl.when((pl.program_id(0) == 0) & (pl.program_id(1) == 0))
    def _():
        _load_weights_bf16([(w_in_hbm, w_in_ref), (w_out_hbm, w_out_ref)], stage, sem)

    @pl.when(pl.program_id(1) == 0)
    def _():
        st_ref[...] = jnp.zeros_like(st_ref)

    h = h_ref[...]
    hb = h.astype(BF16)
    fz = _dot(hb, w_in_ref[:, W:2 * W])
    uv = _dot(hb, w_in_ref[:, 4 * W:])

    lbl = lbl_ref[...]
    lbe = jnp.exp(lbl - jnp.max(lbl, axis=0, keepdims=True))
    lb = lbe[0:1, :] / jnp.sum(lbe, axis=0, keepdims=True)

    e = jnp.exp(-jnp.abs(fz))
    r = 1.0 / (1.0 + e)
    er = e * r
    pos = fz >= 0.0
    f = lb + (1.0 - lb) * jnp.where(pos, r, er)
    k = (1.0 - lb) * jnp.where(pos, er, r)
    lf = jnp.log2(f)

    ri = lax.broadcasted_iota(jnp.int32, (CUMSUM_ROWS, CUMSUM_ROWS), 0)
    ci = lax.broadcasted_iota(jnp.int32, (CUMSUM_ROWS, CUMSUM_ROWS), 1)
    tri = ((ri >= ci) & (ri // HG_CHUNK == ci // HG_CHUNK)).astype(BF16)
    lf0 = lf.astype(BF16)
    rem = lf - lf0.astype(F32)
    lf1 = rem.astype(BF16)
    lf2 = (rem - lf1.astype(F32)).astype(BF16)
    b = jnp.concatenate(
        [_dot(tri, lf0[rs, :]) + _dot(tri, lf1[rs, :]) + _dot(tri, lf2[rs, :])
         for rs in (slice(i, i + CUMSUM_ROWS) for i in range(0, rows, CUMSUM_ROWS))], axis=0)

    q = _dot(_after(hb, [b[i - 16:i, 0:HG_DK] for i in range(CUMSUM_ROWS, rows + 1, CUMSUM_ROWS)]),
             w_in_ref[:, 0:W])
    rw = lax.broadcasted_iota(jnp.int32, (rows, W), 0)
    n_chunks = rows // HG_CHUNK
    blocks = [(hh, c) for hh in range(HG_HEADS) for c in range(n_chunks)]
    col = lambda hh: slice(hh * HG_DK, (hh + 1) * HG_DK)
    row_ = lambda c: slice(c * HG_CHUNK, (c + 1) * HG_CHUNK)
    ti = lax.broadcasted_iota(jnp.int32, (HG_CHUNK, HG_CHUNK), 0)
    si = lax.broadcasted_iota(jnp.int32, (HG_CHUNK, HG_CHUNK), 1)

    uvg = uv * (0.5 * (1.0 + jnp.tanh(0.7978845608028654 * (uv + 0.044715 * (uv * uv * uv)))))
    causal = (lax.broadcasted_iota(jnp.int32, (SG_CHUNK, SG_CHUNK), 0)
              >= lax.broadcasted_iota(jnp.int32, (SG_CHUNK, SG_CHUNK), 1))
    sg_tail = []
    for g in range(SG_GROUPS):
        cs = slice(g * SG_DIM, (g + 1) * SG_DIM)
        u_g = uvg[:, cs]
        v_g = uvg[:, SG_WIDTH + g * SG_DIM:SG_WIDTH + (g + 1) * SG_DIM]
        vn = _layer_norm(v_g, sglg_ref[:, cs], sglb_ref[:, cs]).astype(BF16)
        w_g = jnp.where(causal, ws_ref[g], 0.0).astype(BF16)
        bias = bst_ref[:, g:g + 1]
        for n in range(rows // SG_CHUNK):
            rs = slice(n * SG_CHUNK, (n + 1) * SG_CHUNK)
            s_ = _dot(w_g, vn[rs, :]) + bias
            mix_scr[rs, HG_WIDTH + g * SG_DIM:HG_WIDTH + (g + 1) * SG_DIM] = (u_g[rs, :] * s_).astype(BF16)
        sg_tail.append(s_[0:16, :])
    iv = _dot(_after(hb, sg_tail[:SG_GROUPS // 2]), w_in_ref[:, 2 * W:3 * W])
    gt = _dot(_after(hb, sg_tail[SG_GROUPS // 2:]), w_in_ref[:, 3 * W:4 * W])
    vb16 = iv.astype(BF16)

    b_end = _bcast_row(b, HG_CHUNK, HG_CHUNK - 1)
    q_st = (q * jnp.exp2(b)).astype(BF16)
    k_st = (k * jnp.exp2(b_end - b)).astype(BF16)
    dec_end = jnp.exp2(b_end)
    st_inc = {(hh, c): _dot_tn(vb16[row_(c), col(hh)], k_st[row_(c), col(hh)]) for hh, c in blocks}
    gate_act = gt * _sigmoid(gt)

    x = b - _bcast_row(b, HG_CHUNK, HG_CHUNK // 2 - 1)
    qm = (q * jnp.exp2(x)).astype(BF16)
    km = (k * jnp.exp2(-x)).astype(BF16)
    for i, (hh, c) in enumerate(blocks):
        s_ = _dot_nt(qm[row_(c), col(hh)], km[row_(c), col(hh)])
        sc_scr[i] = jnp.where(ti >= si, s_, 0.0).astype(BF16)
    chunk_ends = b.reshape(n_chunks, HG_CHUNK, W)[:, HG_CHUNK - 1, :]
    safe = jnp.max(-chunk_ends) < HG_SAFE_LOG2

    @pl.when(jnp.logical_not(safe))
    def _():
        lhs = [q.astype(BF16), jnp.where((rw % 2) == 1, q * f, k).astype(BF16)]
        for L in HG_LEVELS[1:]:
            if 2 * L >= 8:
                ref = _bcast_row(b, 2 * L, L - 1)
            else:
                ref = jnp.where((rw % 8) < 4, _bcast_row(b, 8, 1), _bcast_row(b, 8, 5))
            el = jnp.exp2(-jnp.abs(b - ref))
            upper = (rw % (2 * L)) >= L
            lhs.append((jnp.where(upper, q, k) * el).astype(BF16))
        rhs = [k.astype(BF16)] + lhs[1:]
        masks = [ti == si] + [((ti // (2 * L)) == (si // (2 * L))) & ((ti % (2 * L)) >= L) & ((si % (2 * L)) < L)
                              for L in HG_LEVELS]
        for i, (hh, c) in enumerate(blocks):
            s_ = jnp.zeros((HG_CHUNK, HG_CHUNK), F32)
            for a_, b_, m_ in zip(lhs, rhs, masks):
                s_ = jnp.where(m_, _dot_nt(a_[row_(c), col(hh)], b_[row_(c), col(hh)]), s_)
            sc_scr[i] = s_.astype(BF16)

    for hh in range(HG_HEADS):
        st = st_ref[hh]
        for c in range(n_chunks):
            o_scr[row_(c), col(hh)] = (_dot(sc_scr[hh * n_chunks + c], vb16[row_(c), col(hh)])
                                       + _dot_nt(q_st[row_(c), col(hh)], st.astype(BF16)))
            last = (c + 1) * HG_CHUNK - 1
            st = st * dec_end[last:last + 1, col(hh)] + st_inc[hh, c]
        st_ref[hh] = st

    o = o_scr[...]
    hgn = hgn_ref[...]
    for hh in range(HG_HEADS):
        cs = col(hh)
        oh = o[:, cs]
        ms = jnp.mean(oh * oh, axis=-1, keepdims=True)
        mix_scr[:, cs] = (oh * lax.rsqrt(ms + LN_EPS) * hgn * gate_act[:, cs]).astype(BF16)

    for r in range(0, rows, OUT_ROWS):
        rs = slice(r, r + OUT_ROWS)
        mix = _dot(mix_scr[rs, :], w_out_ref[...])
        o_ref[rs, :] = _layer_norm(alpha * h[rs, :] + mix, g_ref[...], b_ref[...])


def _xattn_ffn_kernel(alpha, h_ref, k_ref, v_ref, wq_hbm, wo_hbm, g3_ref, b3_ref,
                      wg_hbm, wu_hbm, wd_hbm, g4_ref, b4_ref, o_ref, a_scr, h_scr,
                      wq_ref, wo_ref, wg_ref, wu_ref, wd_ref, stage, sem):
    @pl.when(pl.program_id(0) == 0)
    def _():
        _load_weights_bf16([(wq_hbm, wq_ref), (wo_hbm, wo_ref), (wg_hbm, wg_ref), (wu_hbm, wu_ref),
                            (wd_hbm, wd_ref)], stage, sem)

    h = h_ref[...]
    d_model = h.shape[1]
    hd = d_model // X_HEADS
    qb = (_dot(h.astype(BF16), wq_ref[...]) * (hd ** -0.5)).astype(BF16)
    for hh in range(X_HEADS):
        cs = slice(hh * hd, (hh + 1) * hd)
        s = _dot_nt(qb[:, cs], k_ref[:, cs])
        p = jnp.exp(s - jnp.max(s, axis=-1, keepdims=True))
        l = jnp.sum(p, axis=-1, keepdims=True)
        a_scr[:, cs] = (_dot(p.astype(BF16), v_ref[:, cs]) / l).astype(BF16)
    groups = [slice(r, r + SUB_ROWS) for r in range(0, h.shape[0], SUB_ROWS)]
    h3 = [_layer_norm(alpha * h[rs, :] + _dot(a_scr[rs, :], wo_ref[...]), g3_ref[...], b3_ref[...])
          for rs in groups]
    for rs, h3_g in zip(groups, h3):
        o_ref[rs, :] = _swiglu_ln(h3_g, alpha, wg_ref, wu_ref, wd_ref, g4_ref, b4_ref, h_scr.at[rs, :])


def _resident(shape):
    return pl.BlockSpec(shape, lambda *_: (0,) * len(shape), pipeline_mode=pl.Buffered(1))


_HBM = pl.BlockSpec(memory_space=pl.ANY)


def _weight_scratch(shapes):
    width = max(c for _, c in shapes)
    return ([pltpu.VMEM(shape, BF16) for shape in shapes]
            + [pltpu.VMEM((LOAD_SLOTS, LOAD_ROWS, width), F32), pltpu.SemaphoreType.DMA((LOAD_SLOTS,))])


def _params(n_grid_axes):
    return pltpu.CompilerParams(dimension_semantics=("arbitrary",) * n_grid_axes,
                                vmem_limit_bytes=VMEM_LIMIT_BYTES)


def kernel(x, mem, ffn1_w_gate, ffn1_w_up, ffn1_w_down, ln1_g, ln1_b, w_in, hg_lb_logits, hg_norm_g, sg_ln_g, sg_ln_b, sg_w_s, sg_b_s, w_out, ln2_g, ln2_b, mem_ln_g, mem_ln_b, xa_w_q, xa_w_k, xa_w_v, xa_w_o, ln3_g, ln3_b, ffn2_w_gate, ffn2_w_up, ffn2_w_down, ln4_g, ln4_b):
    depth = w_in.shape[0]
    assert depth == 1, "single-layer stack"
    B, T, D = x.shape
    n_rows = B * T
    d_ff = ffn1_w_gate.shape[2]
    alpha = (2.0 * depth) ** 0.25
    assert T % MIX_ROWS == 0 and T % FFN_ROWS == 0 and T % FFN1_ROWS == 0 and d_ff % FFN_COLS == 0
    assert MIX_ROWS % SG_CHUNK == 0 and MIX_ROWS % HG_CHUNK == 0
    assert D % LOAD_ROWS == 0 and d_ff % LOAD_ROWS == 0

    row = lambda a: a.reshape(1, -1)
    mat = lambda a: a.reshape(a.shape[1:])
    x2 = x.reshape(n_rows, D)
    mem2 = mem.reshape(B * MEM_LEN, D)

    k_mem, v_mem = pl.pallas_call(
        _kv_kernel,
        grid=(B,),
        in_specs=[pl.BlockSpec((MEM_LEN, D), lambda i: (i, 0)),
                  _resident((1, D)), _resident((1, D)), _HBM, _HBM],
        out_specs=[pl.BlockSpec((MEM_LEN, D), lambda i: (i, 0))] * 2,
        out_shape=[jax.ShapeDtypeStruct((B * MEM_LEN, D), BF16)] * 2,
        scratch_shapes=_weight_scratch([(D, D), (D, D)]),
        compiler_params=_params(1),
        name="memory_kv",
    )(mem2, row(mem_ln_g[0]), row(mem_ln_b[0]), mat(xa_w_k), mat(xa_w_v))

    row1_spec = pl.BlockSpec((FFN1_ROWS, D), lambda i: (i, 0))
    h1 = pl.pallas_call(
        functools.partial(_ffn_kernel, alpha),
        grid=(n_rows // FFN1_ROWS,),
        in_specs=[row1_spec, _HBM, _HBM, _HBM, _resident((1, D)), _resident((1, D))],
        out_specs=row1_spec,
        out_shape=jax.ShapeDtypeStruct((n_rows, D), F32),
        scratch_shapes=[pltpu.VMEM((FFN1_ROWS, d_ff), BF16)]
        + _weight_scratch([(D, d_ff), (D, d_ff), (d_ff, D)]),
        compiler_params=_params(1),
        name="ffn1_ln1",
    )(x2, mat(ffn1_w_gate), mat(ffn1_w_up), mat(ffn1_w_down), row(ln1_g[0]), row(ln1_b[0]))

    tiles_per_seq = T // MIX_ROWS
    in_width = w_in.shape[2]
    mix_spec = pl.BlockSpec((MIX_ROWS, D), lambda bi, ti: (bi * tiles_per_seq + ti, 0))
    h2 = pl.pallas_call(
        functools.partial(_mixer_kernel, alpha),
        grid=(B, tiles_per_seq),
        in_specs=[mix_spec, _HBM,
                  _resident((hg_lb_logits.shape[0], HG_WIDTH)), _resident((1, HG_DK)),
                  _resident((1, SG_WIDTH)), _resident((1, SG_WIDTH)),
                  _resident((SG_GROUPS, SG_CHUNK, SG_CHUNK)), _resident((SG_CHUNK, SG_GROUPS)),
                  _HBM, _resident((1, D)), _resident((1, D))],
        out_specs=mix_spec,
        out_shape=jax.ShapeDtypeStruct((n_rows, D), F32),
        scratch_shapes=[pltpu.VMEM((HG_HEADS, HG_DK, HG_DK), F32),
                        pltpu.VMEM((MIX_ROWS, HG_WIDTH), F32),
                        pltpu.VMEM((MIX_ROWS, D), BF16),
                        pltpu.VMEM((HG_HEADS * (MIX_ROWS // HG_CHUNK), HG_CHUNK, HG_CHUNK), BF16)]
        + _weight_scratch([(D, in_width), (D, D)]),
        compiler_params=_params(2),
        name="mixer_ln2",
    )(h1, mat(w_in), hg_lb_logits.reshape(hg_lb_logits.shape[0], HG_WIDTH), row(hg_norm_g[0]),
      row(sg_ln_g[0]), row(sg_ln_b[0]), sg_w_s[0], sg_b_s[0].T, mat(w_out),
      row(ln2_g[0]), row(ln2_b[0]))

    row_spec = pl.BlockSpec((FFN_ROWS, D), lambda i: (i, 0))
    tiles_per_batch = T // FFN_ROWS
    kv_spec = pl.BlockSpec((MEM_LEN, D), lambda i: (i // tiles_per_batch, 0))
    out = pl.pallas_call(
        functools.partial(_xattn_ffn_kernel, alpha),
        grid=(n_rows // FFN_ROWS,),
        in_specs=[row_spec, kv_spec, kv_spec, _HBM, _HBM, _resident((1, D)), _resident((1, D)),
                  _HBM, _HBM, _HBM, _resident((1, D)), _resident((1, D))],
        out_specs=row_spec,
        out_shape=jax.ShapeDtypeStruct((n_rows, D), F32),
        scratch_shapes=[pltpu.VMEM((FFN_ROWS, D), BF16), pltpu.VMEM((FFN_ROWS, d_ff), BF16)]
        + _weight_scratch([(D, D), (D, D), (D, d_ff), (D, d_ff), (d_ff, D)]),
        compiler_params=_params(1),
        name="xattn_ln3_ffn2_ln4",
    )(h2, k_mem, v_mem, mat(xa_w_q), mat(xa_w_o), row(ln3_g[0]), row(ln3_b[0]),
      mat(ffn2_w_gate), mat(ffn2_w_up), mat(ffn2_w_down), row(ln4_g[0]), row(ln4_b[0]))

    return out.reshape(B, T, D)
```

```python
import functools

import jax
import jax.numpy as jnp
from jax import lax
from jax.experimental import pallas as pl
from jax.experimental.pallas import tpu as pltpu

F32 = jnp.float32
BF16 = jnp.bfloat16

LN_EPS = 1e-5
GELU_SQRT_2_OVER_PI = 0.7978845608028654
GELU_CUBIC = 0.044715
MEM_LEN = 256
HG_HEADS = 4
HG_DK = 128
HG_WIDTH = HG_HEADS * HG_DK
SG_GROUPS = 4
SG_DIM = 128
SG_WIDTH = SG_GROUPS * SG_DIM
SG_CHUNK = 128
X_HEADS = 4

HG_CHUNK = 64
HG_LEVELS = (1, 2, 4, 8, 16, 32)
HG_SAFE_LOG2 = 100.0

FFN1_ROWS = 1024
FFN_ROWS = 512
SUB_ROWS = 256
FFN_COLS = 256
MIX_ROWS = 512
OUT_ROWS = 128
LOAD_ROWS = 128
LOAD_SLOTS = 4
VMEM_LIMIT_BYTES = 56 * 1024 * 1024
LANES = 128
BF16_TILE_ROWS = 16


def _dot(a, b):
    return jnp.dot(a, b, preferred_element_type=F32)


def _dot_nt(a, b):
    return lax.dot_general(a, b, (((1,), (1,)), ((), ())), preferred_element_type=F32)


def _dot_tn(a, b):
    return lax.dot_general(a, b, (((0,), (0,)), ((), ())), preferred_element_type=F32)


def _layer_norm(y, g, b):
    mu = jnp.mean(y, axis=-1, keepdims=True)
    d = y - mu
    var = jnp.mean(d * d, axis=-1, keepdims=True)
    return d * lax.rsqrt(var + LN_EPS) * g + b


def _sigmoid(x):
    return 1.0 / (1.0 + jnp.exp(-x))


def _load_weights_bf16(pairs, stage, sem):
    chunks = [(src, dst, r) for src, dst in pairs for r in range(0, src.shape[0], LOAD_ROWS)]
    ahead = LOAD_SLOTS - 1

    def chunk_copy(i):
        src, _, r = chunks[i]
        slot = i % LOAD_SLOTS
        return pltpu.make_async_copy(src.at[pl.ds(r, LOAD_ROWS), :],
                                     stage.at[slot, :, pl.ds(0, src.shape[1])], sem.at[slot])

    for i in range(min(ahead, len(chunks))):
        chunk_copy(i).start()
    for i, (src, dst, r) in enumerate(chunks):
        if i + ahead < len(chunks):
            chunk_copy(i + ahead).start()
        chunk_copy(i).wait()
        dst[pl.ds(r, LOAD_ROWS), :] = stage[i % LOAD_SLOTS, :, 0:src.shape[1]].astype(BF16)


def _after(x, deps):
    always = None
    for d in deps:
        c = (d == d) | (d != d)
        always = c if always is None else always & c
    t = jnp.where(always, -jnp.inf, deps[0]).astype(x.dtype)
    return jnp.maximum(x, jnp.tile(t, (x.shape[0] // BF16_TILE_ROWS, x.shape[1] // LANES)))


def _swiglu_ln(x, alpha, wg_ref, wu_ref, wd_ref, g_ref, b_ref, h_scr):
    xb = x.astype(BF16)
    d_ff = wg_ref.shape[1]
    for j in range(d_ff // FFN_COLS):
        sl = slice(j * FFN_COLS, (j + 1) * FFN_COLS)
        gate = _dot(xb, wg_ref[:, sl])
        up = _dot(xb, wu_ref[:, sl])
        h_scr[:, sl] = (gate * _sigmoid(gate) * up).astype(BF16)
    y = _dot(h_scr[...], wd_ref[...])
    return _layer_norm(alpha * x + 0.5 * y, g_ref[...], b_ref[...])


def _kv_kernel(mem_ref, g_ref, b_ref, wk_hbm, wv_hbm, k_ref, v_ref, wk_ref, wv_ref, stage, sem):
    @pl.when(pl.program_id(0) == 0)
    def _():
        _load_weights_bf16([(wk_hbm, wk_ref), (wv_hbm, wv_ref)], stage, sem)

    m = _layer_norm(mem_ref[...], g_ref[...], b_ref[...]).astype(BF16)
    k_ref[...] = _dot(m, wk_ref[...]).astype(BF16)
    v_ref[...] = _dot(m, wv_ref[...]).astype(BF16)


def _ffn_kernel(alpha, x_ref, wg_hbm, wu_hbm, wd_hbm, g_ref, b_ref, o_ref, h_scr,
                wg_ref, wu_ref, wd_ref, stage, sem):
    @pl.when(pl.program_id(0) == 0)
    def _():
        _load_weights_bf16([(wg_hbm, wg_ref), (wu_hbm, wu_ref), (wd_hbm, wd_ref)], stage, sem)

    for r in range(0, x_ref.shape[0], SUB_ROWS):
        rs = slice(r, r + SUB_ROWS)
        o_ref[rs, :] = _swiglu_ln(x_ref[rs, :], alpha, wg_ref, wu_ref, wd_ref, g_ref, b_ref, h_scr.at[rs, :])


def _bcast_row(a, blk, r):
    rows, w = a.shape
    a3 = a.reshape(rows // blk, blk, w)
    return jnp.broadcast_to(a3[:, r:r + 1, :], a3.shape).reshape(rows, w)


def _mixer_kernel(alpha, h_ref, w_in_hbm, lbl_ref, hgn_ref, sglg_ref, sglb_ref, ws_ref,
                  bst_ref, w_out_hbm, g_ref, b_ref, o_ref, st_ref, o_scr, mix_scr, sc_scr,
                  w_in_ref, w_out_ref, stage, sem):
    rows = h_ref.shape[0]
    W = HG_WIDTH

    @pl.when((pl.program_id(0) == 0) & (pl.program_id(1) == 0))
    def _():
        _load_weights_bf16([(w_in_hbm, w_in_ref), (w_out_hbm, w_out_ref)], stage, sem)

    @pl.when(pl.program_id(1) == 0)
    def _():
        st_ref[...] = jnp.zeros_like(st_ref)

    h = h_ref[...]
    hb = h.astype(BF16)
    fz = _dot(hb, w_in_ref[:, W:2 * W])
    uv = _dot(hb, w_in_ref[:, 4 * W:])

    lbl = lbl_ref[...]
    lbe = jnp.exp(lbl - jnp.max(lbl, axis=0, keepdims=True))
    lb = lbe[0:1, :] / jnp.sum(lbe, axis=0, keepdims=True)

    e = jnp.exp(-jnp.abs(fz))
    r = 1.0 / (1.0 + e)
    er = e * r
    pos = fz >= 0.0
    f = lb + (1.0 - lb) * jnp.where(pos, r, er)
    k = (1.0 - lb) * jnp.where(pos, er, r)
    lf = jnp.log2(f)

    row_in_chunk = lax.broadcasted_iota(jnp.int32, (rows, W), 0) % HG_CHUNK
    b = lf
    step = 1
    while step < HG_CHUNK:
        b = b + jnp.where(row_in_chunk >= step, pltpu.roll(b, step, axis=0), 0.0)
        step *= 2

    q = _dot(_after(hb, [b[i - BF16_TILE_ROWS:i, hh * HG_DK:(hh + 1) * HG_DK]
                         for i in range(HG_CHUNK, rows + 1, HG_CHUNK) for hh in range(HG_HEADS)]),
             w_in_ref[:, 0:W])
    rw = lax.broadcasted_iota(jnp.int32, (rows, W), 0)
    n_chunks = rows // HG_CHUNK
    blocks = [(hh, c) for hh in range(HG_HEADS) for c in range(n_chunks)]
    col = lambda hh: slice(hh * HG_DK, (hh + 1) * HG_DK)
    row_ = lambda c: slice(c * HG_CHUNK, (c + 1) * HG_CHUNK)
    ti = lax.broadcasted_iota(jnp.int32, (HG_CHUNK, HG_CHUNK), 0)
    si = lax.broadcasted_iota(jnp.int32, (HG_CHUNK, HG_CHUNK), 1)

    uvg = uv * (0.5 * (1.0 + jnp.tanh(GELU_SQRT_2_OVER_PI * (uv + GELU_CUBIC * (uv * uv * uv)))))
    causal = (lax.broadcasted_iota(jnp.int32, (SG_CHUNK, SG_CHUNK), 0)
              >= lax.broadcasted_iota(jnp.int32, (SG_CHUNK, SG_CHUNK), 1))
    sg_tail = []
    for g in range(SG_GROUPS):
        cs = slice(g * SG_DIM, (g + 1) * SG_DIM)
        u_g = uvg[:, cs]
        v_g = uvg[:, SG_WIDTH + g * SG_DIM:SG_WIDTH + (g + 1) * SG_DIM]
        vn = _layer_norm(v_g, sglg_ref[:, cs], sglb_ref[:, cs]).astype(BF16)
        w_g = jnp.where(causal, ws_ref[g], 0.0).astype(BF16)
        bias = bst_ref[:, g:g + 1]
        for n in range(rows // SG_CHUNK):
            rs = slice(n * SG_CHUNK, (n + 1) * SG_CHUNK)
            s_ = _dot(w_g, vn[rs, :]) + bias
            mix_scr[rs, HG_WIDTH + g * SG_DIM:HG_WIDTH + (g + 1) * SG_DIM] = (u_g[rs, :] * s_).astype(BF16)
        sg_tail.append(s_[0:BF16_TILE_ROWS, :])
    iv = _dot(_after(hb, sg_tail[:SG_GROUPS // 2]), w_in_ref[:, 2 * W:3 * W])
    chunk_ends = b.reshape(rows // HG_CHUNK, HG_CHUNK, W)[:, HG_CHUNK - 1, :]
    worst = jnp.max(jnp.max(-chunk_ends, axis=0, keepdims=True), axis=1, keepdims=True)
    gt = _dot(_after(hb, sg_tail[SG_GROUPS // 2:] + [jnp.broadcast_to(worst, (BF16_TILE_ROWS, LANES))]),
              w_in_ref[:, 3 * W:4 * W])
    vb16 = iv.astype(BF16)

    b_end = _bcast_row(b, HG_CHUNK, HG_CHUNK - 1)
    q_st = (q * jnp.exp2(b)).astype(BF16)
    k_st = (k * jnp.exp2(b_end - b)).astype(BF16)
    dec_end = jnp.exp2(b_end)
    st_inc = {(hh, c): _dot_tn(vb16[row_(c), col(hh)], k_st[row_(c), col(hh)]) for hh, c in blocks}
    gate_act = gt * _sigmoid(gt)

    x = b - _bcast_row(b, HG_CHUNK, HG_CHUNK // 2 - 1)
    qm = (q * jnp.exp2(x)).astype(BF16)
    km = (k * jnp.exp2(-x)).astype(BF16)
    for i, (hh, c) in enumerate(blocks):
        s_ = _dot_nt(qm[row_(c), col(hh)], km[row_(c), col(hh)])
        sc_scr[i] = jnp.where(ti >= si, s_, 0.0).astype(BF16)
    safe = worst[0, 0] < HG_SAFE_LOG2

    @pl.when(jnp.logical_not(safe))
    def _():
        lhs = [q.astype(BF16), jnp.where((rw % 2) == 1, q * f, k).astype(BF16)]
        for L in HG_LEVELS[1:]:
            if 2 * L >= 8:
                ref = _bcast_row(b, 2 * L, L - 1)
            else:
                ref = jnp.where((rw % 8) < 4, _bcast_row(b, 8, 1), _bcast_row(b, 8, 5))
            el = jnp.exp2(-jnp.abs(b - ref))
            upper = (rw % (2 * L)) >= L
            lhs.append((jnp.where(upper, q, k) * el).astype(BF16))
        rhs = [k.astype(BF16)] + lhs[1:]
        masks = [ti == si] + [((ti // (2 * L)) == (si // (2 * L))) & ((ti % (2 * L)) >= L) & ((si % (2 * L)) < L)
                              for L in HG_LEVELS]
        for i, (hh, c) in enumerate(blocks):
            s_ = jnp.zeros((HG_CHUNK, HG_CHUNK), F32)
            for a_, b_, m_ in zip(lhs, rhs, masks):
                s_ = jnp.where(m_, _dot_nt(a_[row_(c), col(hh)], b_[row_(c), col(hh)]), s_)
            sc_scr[i] = s_.astype(BF16)

    for hh in range(HG_HEADS):
        st = st_ref[hh]
        for c in range(n_chunks):
            o_scr[row_(c), col(hh)] = (_dot(sc_scr[hh * n_chunks + c], vb16[row_(c), col(hh)])
                                       + _dot_nt(q_st[row_(c), col(hh)], st.astype(BF16)))
            last = (c + 1) * HG_CHUNK - 1
            st = st * dec_end[last:last + 1, col(hh)] + st_inc[hh, c]
        st_ref[hh] = st

    o = o_scr[...]
    hgn = hgn_ref[...]
    for hh in range(HG_HEADS):
        cs = col(hh)
        oh = o[:, cs]
        ms = jnp.mean(oh * oh, axis=-1, keepdims=True)
        mix_scr[:, cs] = (oh * lax.rsqrt(ms + LN_EPS) * hgn * gate_act[:, cs]).astype(BF16)

    for r in range(0, rows, OUT_ROWS):
        rs = slice(r, r + OUT_ROWS)
        mix = _dot(mix_scr[rs, :], w_out_ref[...])
        o_ref[rs, :] = _layer_norm(alpha * h[rs, :] + mix, g_ref[...], b_ref[...])


def _xattn_ffn_kernel(alpha, h_ref, k_ref, v_ref, wq_hbm, wo_hbm, g3_ref, b3_ref,
                      wg_hbm, wu_hbm, wd_hbm, g4_ref, b4_ref, o_ref, a_scr, h_scr,
                      wq_ref, wo_ref, wg_ref, wu_ref, wd_ref, stage, sem):
    @pl.when(pl.program_id(0) == 0)
    def _():
        _load_weights_bf16([(wq_hbm, wq_ref), (wo_hbm, wo_ref), (wg_hbm, wg_ref), (wu_hbm, wu_ref),
                            (wd_hbm, wd_ref)], stage, sem)

    h = h_ref[...]
    d_model = h.shape[1]
    hd = d_model // X_HEADS
    qb = (_dot(h.astype(BF16), wq_ref[...]) * (hd ** -0.5)).astype(BF16)
    for hh in range(X_HEADS):
        cs = slice(hh * hd, (hh + 1) * hd)
        s = _dot_nt(qb[:, cs], k_ref[:, cs])
        p = jnp.exp(s - jnp.max(s, axis=-1, keepdims=True))
        l = jnp.sum(p, axis=-1, keepdims=True)
        a_scr[:, cs] = (_dot(p.astype(BF16), v_ref[:, cs]) / l).astype(BF16)
    groups = [slice(r, r + SUB_ROWS) for r in range(0, h.shape[0], SUB_ROWS)]
    h3 = [_layer_norm(alpha * h[rs, :] + _dot(a_scr[rs, :], wo_ref[...]), g3_ref[...], b3_ref[...])
          for rs in groups]
    for rs, h3_g in zip(groups, h3):
        o_ref[rs, :] = _swiglu_ln(h3_g, alpha, wg_ref, wu_ref, wd_ref, g4_ref, b4_ref, h_scr.at[rs, :])


def _resident(shape):
    return pl.BlockSpec(shape, lambda *_: (0,) * len(shape), pipeline_mode=pl.Buffered(1))


_HBM = pl.BlockSpec(memory_space=pl.ANY)


def _weight_scratch(shapes):
    width = max(c for _, c in shapes)
    return ([pltpu.VMEM(shape, BF16) for shape in shapes]
            + [pltpu.VMEM((LOAD_SLOTS, LOAD_ROWS, width), F32), pltpu.SemaphoreType.DMA((LOAD_SLOTS,))])


def _params(n_grid_axes):
    return pltpu.CompilerParams(dimension_semantics=("arbitrary",) * n_grid_axes,
                                vmem_limit_bytes=VMEM_LIMIT_BYTES)


def kernel(x, mem, ffn1_w_gate, ffn1_w_up, ffn1_w_down, ln1_g, ln1_b, w_in, hg_lb_logits, hg_norm_g, sg_ln_g, sg_ln_b, sg_w_s, sg_b_s, w_out, ln2_g, ln2_b, mem_ln_g, mem_ln_b, xa_w_q, xa_w_k, xa_w_v, xa_w_o, ln3_g, ln3_b, ffn2_w_gate, ffn2_w_up, ffn2_w_down, ln4_g, ln4_b):
    depth = w_in.shape[0]
    assert depth == 1, "single-layer stack"
    B, T, D = x.shape
    n_rows = B * T
    d_ff = ffn1_w_gate.shape[2]
    alpha = (2.0 * depth) ** 0.25
    assert T % MIX_ROWS == 0 and T % FFN_ROWS == 0 and T % FFN1_ROWS == 0 and d_ff % FFN_COLS == 0
    assert MIX_ROWS % SG_CHUNK == 0 and MIX_ROWS % HG_CHUNK == 0
    assert D % LOAD_ROWS == 0 and d_ff % LOAD_ROWS == 0

    row = lambda a: a.reshape(1, -1)
    mat = lambda a: a.reshape(a.shape[1:])
    x2 = x.reshape(n_rows, D)
    mem2 = mem.reshape(B * MEM_LEN, D)

    k_mem, v_mem = pl.pallas_call(
        _kv_kernel,
        grid=(B,),
        in_specs=[pl.BlockSpec((MEM_LEN, D), lambda i: (i, 0)),
                  _resident((1, D)), _resident((1, D)), _HBM, _HBM],
        out_specs=[pl.BlockSpec((MEM_LEN, D), lambda i: (i, 0))] * 2,
        out_shape=[jax.ShapeDtypeStruct((B * MEM_LEN, D), BF16)] * 2,
        scratch_shapes=_weight_scratch([(D, D), (D, D)]),
        compiler_params=_params(1),
        name="memory_kv",
    )(mem2, row(mem_ln_g[0]), row(mem_ln_b[0]), mat(xa_w_k), mat(xa_w_v))

    row1_spec = pl.BlockSpec((FFN1_ROWS, D), lambda i: (i, 0))
    h1 = pl.pallas_call(
        functools.partial(_ffn_kernel, alpha),
        grid=(n_rows // FFN1_ROWS,),
        in_specs=[row1_spec, _HBM, _HBM, _HBM, _resident((1, D)), _resident((1, D))],
        out_specs=row1_spec,
        out_shape=jax.ShapeDtypeStruct((n_rows, D), F32),
        scratch_shapes=[pltpu.VMEM((FFN1_ROWS, d_ff), BF16)]
        + _weight_scratch([(D, d_ff), (D, d_ff), (d_ff, D)]),
        compiler_params=_params(1),
        name="ffn1_ln1",
    )(x2, mat(ffn1_w_gate), mat(ffn1_w_up), mat(ffn1_w_down), row(ln1_g[0]), row(ln1_b[0]))

    tiles_per_seq = T // MIX_ROWS
    in_width = w_in.shape[2]
    mix_spec = pl.BlockSpec((MIX_ROWS, D), lambda bi, ti: (bi * tiles_per_seq + ti, 0))
    h2 = pl.pallas_call(
        functools.partial(_mixer_kernel, alpha),
        grid=(B, tiles_per_seq),
        in_specs=[mix_spec, _HBM,
                  _resident((hg_lb_logits.shape[0], HG_WIDTH)), _resident((1, HG_DK)),
                  _resident((1, SG_WIDTH)), _resident((1, SG_WIDTH)),
                  _resident((SG_GROUPS, SG_CHUNK, SG_CHUNK)), _resident((SG_CHUNK, SG_GROUPS)),
                  _HBM, _resident((1, D)), _resident((1, D))],
        out_specs=mix_spec,
        out_shape=jax.ShapeDtypeStruct((n_rows, D), F32),
        scratch_shapes=[pltpu.VMEM((HG_HEADS, HG_DK, HG_DK), F32),
                        pltpu.VMEM((MIX_ROWS, HG_WIDTH), F32),
                        pltpu.VMEM((MIX_ROWS, D), BF16),
                        pltpu.VMEM((HG_HEADS * (MIX_ROWS // HG_CHUNK), HG_CHUNK, HG_CHUNK), BF16)]
        + _weight_scratch([(D, in_width), (D, D)]),
        compiler_params=_params(2),
        name="mixer_ln2",
    )(h1, mat(w_in), hg_lb_logits.reshape(hg_lb_logits.shape[0], HG_WIDTH), row(hg_norm_g[0]),
      row(sg_ln_g[0]), row(sg_ln_b[0]), sg_w_s[0], sg_b_s[0].T, mat(w_out),
      row(ln2_g[0]), row(ln2_b[0]))

    row_spec = pl.BlockSpec((FFN_ROWS, D), lambda i: (i, 0))
    tiles_per_batch = T // FFN_ROWS
    kv_spec = pl.BlockSpec((MEM_LEN, D), lambda i: (i // tiles_per_batch, 0))
    out = pl.pallas_call(
        functools.partial(_xattn_ffn_kernel, alpha),
        grid=(n_rows // FFN_ROWS,),
        in_specs=[row_spec, kv_spec, kv_spec, _HBM, _HBM, _resident((1, D)), _resident((1, D)),
                  _HBM, _HBM, _HBM, _resident((1, D)), _resident((1, D))],
        out_specs=row_spec,
        out_shape=jax.ShapeDtypeStruct((n_rows, D), F32),
        scratch_shapes=[pltpu.VMEM((FFN_ROWS, D), BF16), pltpu.VMEM((FFN_ROWS, d_ff), BF16)]
        + _weight_scratch([(D, D), (D, D), (D, d_ff), (D, d_ff), (d_ff, D)]),
        compiler_params=_params(1),
        name="xattn_ln3_ffn2_ln4",
    )(h2, k_mem, v_mem, mat(xa_w_q), mat(xa_w_o), row(ln3_g[0]), row(ln3_b[0]),
      mat(ffn2_w_gate), mat(ffn2_w_up), mat(ffn2_w_down), row(ln4_g[0]), row(ln4_b[0]))

    return out.reshape(B, T, D)
```

```python
import functools

import jax
import jax.numpy as jnp
from jax import lax
from jax.experimental import pallas as pl
from jax.experimental.pallas import tpu as pltpu

F32 = jnp.float32
BF16 = jnp.bfloat16

LN_EPS = 1e-5
GELU_SQRT_2_OVER_PI = 0.7978845608028654
GELU_CUBIC = 0.044715
MEM_LEN = 256
HG_HEADS = 4
HG_DK = 128
HG_WIDTH = HG_HEADS * HG_DK
SG_GROUPS = 4
SG_DIM = 128
SG_WIDTH = SG_GROUPS * SG_DIM
SG_CHUNK = 128
X_HEADS = 4

HG_CHUNK = 64
HG_LEVELS = (1, 2, 4, 8, 16, 32)
HG_SAFE_LOG2 = 100.0

FFN1_ROWS = 1024
FFN_ROWS = 1024
SUB_ROWS = 256
FFN_COLS = 256
MIX_ROWS = 512
OUT_ROWS = 128
LOAD_ROWS = 128
LOAD_SLOTS = 4
VMEM_LIMIT_BYTES = 56 * 1024 * 1024
LANES = 128
BF16_TILE_ROWS = 16


def _dot(a, b):
    return jnp.dot(a, b, preferred_element_type=F32)


def _dot_nt(a, b):
    return lax.dot_general(a, b, (((1,), (1,)), ((), ())), preferred_element_type=F32)


def _dot_tn(a, b):
    return lax.dot_general(a, b, (((0,), (0,)), ((), ())), preferred_element_type=F32)


def _layer_norm(y, g, b):
    mu = jnp.mean(y, axis=-1, keepdims=True)
    d = y - mu
    var = jnp.mean(d * d, axis=-1, keepdims=True)
    return d * lax.rsqrt(var + LN_EPS) * g + b


def _sigmoid(x):
    return 1.0 / (1.0 + jnp.exp(-x))


def _load_weights_bf16(pairs, stage, sem):
    chunks = [(src, dst, r) for src, dst in pairs for r in range(0, src.shape[0], LOAD_ROWS)]
    ahead = LOAD_SLOTS - 1

    def chunk_copy(i):
        src, _, r = chunks[i]
        slot = i % LOAD_SLOTS
        return pltpu.make_async_copy(src.at[pl.ds(r, LOAD_ROWS), :],
                                     stage.at[slot, :, pl.ds(0, src.shape[1])], sem.at[slot])

    for i in range(min(ahead, len(chunks))):
        chunk_copy(i).start()
    for i, (src, dst, r) in enumerate(chunks):
        if i + ahead < len(chunks):
            chunk_copy(i + ahead).start()
        chunk_copy(i).wait()
        dst[pl.ds(r, LOAD_ROWS), :] = stage[i % LOAD_SLOTS, :, 0:src.shape[1]].astype(BF16)


def _after(x, deps):
    always = None
    for d in deps:
        c = (d == d) | (d != d)
        always = c if always is None else always & c
    t = jnp.where(always, -jnp.inf, deps[0]).astype(x.dtype)
    return jnp.maximum(x, jnp.tile(t, (x.shape[0] // BF16_TILE_ROWS, x.shape[1] // LANES)))


def _swiglu_ln(x, alpha, wg_ref, wu_ref, wd_ref, g_ref, b_ref, h_scr):
    xb = x.astype(BF16)
    d_ff = wg_ref.shape[1]
    for j in range(d_ff // FFN_COLS):
        sl = slice(j * FFN_COLS, (j + 1) * FFN_COLS)
        gate = _dot(xb, wg_ref[:, sl])
        up = _dot(xb, wu_ref[:, sl])
        h_scr[:, sl] = (gate * _sigmoid(gate) * up).astype(BF16)
    y = _dot(h_scr[...], wd_ref[...])
    return _layer_norm(alpha * x + 0.5 * y, g_ref[...], b_ref[...])


def _kv_kernel(mem_ref, g_ref, b_ref, wk_hbm, wv_hbm, k_ref, v_ref, wk_ref, wv_ref, stage, sem):
    @pl.when(pl.program_id(0) == 0)
    def _():
        _load_weights_bf16([(wk_hbm, wk_ref), (wv_hbm, wv_ref)], stage, sem)

    m = _layer_norm(mem_ref[...], g_ref[...], b_ref[...]).astype(BF16)
    k_ref[...] = _dot(m, wk_ref[...]).astype(BF16)
    v_ref[...] = _dot(m, wv_ref[...]).astype(BF16)


def _ffn_kernel(alpha, x_ref, wg_hbm, wu_hbm, wd_hbm, g_ref, b_ref, o_ref, h_scr,
                wg_ref, wu_ref, wd_ref, stage, sem):
    @pl.when(pl.program_id(0) == 0)
    def _():
        _load_weights_bf16([(wg_hbm, wg_ref), (wu_hbm, wu_ref), (wd_hbm, wd_ref)], stage, sem)

    for r in range(0, x_ref.shape[0], SUB_ROWS):
        rs = slice(r, r + SUB_ROWS)
        o_ref[rs, :] = _swiglu_ln(x_ref[rs, :], alpha, wg_ref, wu_ref, wd_ref, g_ref, b_ref, h_scr.at[rs, :])


def _bcast_row(a, blk, r):
    rows, w = a.shape
    a3 = a.reshape(rows // blk, blk, w)
    return jnp.broadcast_to(a3[:, r:r + 1, :], a3.shape).reshape(rows, w)


def _mixer_kernel(alpha, h_ref, w_in_hbm, lbl_ref, hgn_ref, sglg_ref, sglb_ref, ws_ref,
                  bst_ref, w_out_hbm, g_ref, b_ref, o_ref, st_ref, o_scr, mix_scr, sc_scr,
                  w_in_ref, w_out_ref, stage, sem):
    rows = h_ref.shape[0]
    W = HG_WIDTH

    @pl.when((pl.program_id(0) == 0) & (pl.program_id(1) == 0))
    def _():
        _load_weights_bf16([(w_in_hbm, w_in_ref), (w_out_hbm, w_out_ref)], stage, sem)

    @pl.when(pl.program_id(1) == 0)
    def _():
        st_ref[...] = jnp.zeros_like(st_ref)

    h = h_ref[...]
    hb = h.astype(BF16)
    fz = _dot(hb, w_in_ref[:, W:2 * W])
    uv = _dot(hb, w_in_ref[:, 4 * W:])

    lbl = lbl_ref[...]
    lbe = jnp.exp(lbl - jnp.max(lbl, axis=0, keepdims=True))
    lb = lbe[0:1, :] / jnp.sum(lbe, axis=0, keepdims=True)

    n_chunks = rows // HG_CHUNK
    blocks = [(hh, c) for hh in range(HG_HEADS) for c in range(n_chunks)]
    col = lambda hh: slice(hh * HG_DK, (hh + 1) * HG_DK)
    row_ = lambda c: slice(c * HG_CHUNK, (c + 1) * HG_CHUNK)

    rc = lax.broadcasted_iota(jnp.int32, (HG_CHUNK, W), 0)
    f_c, k_c, b_c = [], [], []
    for c in range(n_chunks):
        fzc = fz[row_(c), :]
        e = jnp.exp(-jnp.abs(fzc))
        r = 1.0 / (1.0 + e)
        er = e * r
        pos = fzc >= 0.0
        f = lb + (1.0 - lb) * jnp.where(pos, r, er)
        k_c.append((1.0 - lb) * jnp.where(pos, er, r))
        bc = jnp.log2(f)
        step = 1
        while step < HG_CHUNK:
            bc = bc + jnp.where(rc >= step, pltpu.roll(bc, step, axis=0), 0.0)
            step *= 2
        f_c.append(f)
        b_c.append(bc)

    q = _dot(_after(hb, [bc[HG_CHUNK - BF16_TILE_ROWS:, col(hh)] for bc in b_c for hh in range(HG_HEADS)]),
             w_in_ref[:, 0:W])
    ti = lax.broadcasted_iota(jnp.int32, (HG_CHUNK, HG_CHUNK), 0)
    si = lax.broadcasted_iota(jnp.int32, (HG_CHUNK, HG_CHUNK), 1)

    uvg = uv * (0.5 * (1.0 + jnp.tanh(GELU_SQRT_2_OVER_PI * (uv + GELU_CUBIC * (uv * uv * uv)))))
    causal = (lax.broadcasted_iota(jnp.int32, (SG_CHUNK, SG_CHUNK), 0)
              >= lax.broadcasted_iota(jnp.int32, (SG_CHUNK, SG_CHUNK), 1))
    sg_tail = []
    for g in range(SG_GROUPS):
        cs = slice(g * SG_DIM, (g + 1) * SG_DIM)
        u_g = uvg[:, cs]
        v_g = uvg[:, SG_WIDTH + g * SG_DIM:SG_WIDTH + (g + 1) * SG_DIM]
        vn = _layer_norm(v_g, sglg_ref[:, cs], sglb_ref[:, cs]).astype(BF16)
        w_g = jnp.where(causal, ws_ref[g], 0.0).astype(BF16)
        bias = bst_ref[:, g:g + 1]
        for n in range(rows // SG_CHUNK):
            rs = slice(n * SG_CHUNK, (n + 1) * SG_CHUNK)
            s_ = _dot(w_g, vn[rs, :]) + bias
            mix_scr[rs, HG_WIDTH + g * SG_DIM:HG_WIDTH + (g + 1) * SG_DIM] = (u_g[rs, :] * s_).astype(BF16)
        sg_tail.append(s_[0:BF16_TILE_ROWS, :])
    iv = _dot(_after(hb, sg_tail[:SG_GROUPS // 2]), w_in_ref[:, 2 * W:3 * W])
    gt = _dot(_after(hb, sg_tail[SG_GROUPS // 2:]), w_in_ref[:, 3 * W:4 * W])
    vb16 = iv.astype(BF16)

    gate_act = gt * _sigmoid(gt)

    q_st, dec_end, st_inc = [], [], {}
    worst = None
    for c in range(n_chunks):
        qc, kc, bc = q[row_(c), :], k_c[c], b_c[c]
        b_last = bc[HG_CHUNK - 1:HG_CHUNK, :]
        worst = -b_last if worst is None else jnp.maximum(worst, -b_last)
        q_st.append((qc * jnp.exp2(bc)).astype(BF16))
        k_st = (kc * jnp.exp2(b_last - bc)).astype(BF16)
        dec_end.append(jnp.exp2(b_last))
        x = bc - bc[HG_CHUNK // 2 - 1:HG_CHUNK // 2, :]
        qm = (qc * jnp.exp2(x)).astype(BF16)
        km = (kc * jnp.exp2(-x)).astype(BF16)
        for hh in range(HG_HEADS):
            st_inc[hh, c] = _dot_tn(vb16[row_(c), col(hh)], k_st[:, col(hh)])
            s_ = _dot_nt(qm[:, col(hh)], km[:, col(hh)])
            sc_scr[hh * n_chunks + c] = jnp.where(ti >= si, s_, 0.0).astype(BF16)
    safe = jnp.max(worst) < HG_SAFE_LOG2

    @pl.when(jnp.logical_not(safe))
    def _():
        k, f, b = (jnp.concatenate(parts, axis=0) for parts in (k_c, f_c, b_c))
        rw = lax.broadcasted_iota(jnp.int32, (rows, W), 0)
        lhs = [q.astype(BF16), jnp.where((rw % 2) == 1, q * f, k).astype(BF16)]
        for L in HG_LEVELS[1:]:
            if 2 * L >= 8:
                ref = _bcast_row(b, 2 * L, L - 1)
            else:
                ref = jnp.where((rw % 8) < 4, _bcast_row(b, 8, 1), _bcast_row(b, 8, 5))
            el = jnp.exp2(-jnp.abs(b - ref))
            upper = (rw % (2 * L)) >= L
            lhs.append((jnp.where(upper, q, k) * el).astype(BF16))
        rhs = [k.astype(BF16)] + lhs[1:]
        masks = [ti == si] + [((ti // (2 * L)) == (si // (2 * L))) & ((ti % (2 * L)) >= L) & ((si % (2 * L)) < L)
                              for L in HG_LEVELS]
        for i, (hh, c) in enumerate(blocks):
            s_ = jnp.zeros((HG_CHUNK, HG_CHUNK), F32)
            for a_, b_, m_ in zip(lhs, rhs, masks):
                s_ = jnp.where(m_, _dot_nt(a_[row_(c), col(hh)], b_[row_(c), col(hh)]), s_)
            sc_scr[i] = s_.astype(BF16)

    for hh in range(HG_HEADS):
        st = st_ref[hh]
        for c in range(n_chunks):
            o_scr[row_(c), col(hh)] = (_dot(sc_scr[hh * n_chunks + c], vb16[row_(c), col(hh)])
                                       + _dot_nt(q_st[c][:, col(hh)], st.astype(BF16)))
            st = st * dec_end[c][:, col(hh)] + st_inc[hh, c]
        st_ref[hh] = st

    o = o_scr[...]
    hgn = hgn_ref[...]
    for hh in range(HG_HEADS):
        cs = col(hh)
        oh = o[:, cs]
        ms = jnp.mean(oh * oh, axis=-1, keepdims=True)
        mix_scr[:, cs] = (oh * lax.rsqrt(ms + LN_EPS) * hgn * gate_act[:, cs]).astype(BF16)

    for r in range(0, rows, OUT_ROWS):
        rs = slice(r, r + OUT_ROWS)
        mix = _dot(mix_scr[rs, :], w_out_ref[...])
        o_ref[rs, :] = _layer_norm(alpha * h[rs, :] + mix, g_ref[...], b_ref[...])


def _xattn_ffn_kernel(alpha, h_ref, k_ref, v_ref, wq_hbm, wo_hbm, g3_ref, b3_ref,
                      wg_hbm, wu_hbm, wd_hbm, g4_ref, b4_ref, o_ref, a_scr, h_scr,
                      wq_ref, wo_ref, wg_ref, wu_ref, wd_ref, stage, sem):
    @pl.when(pl.program_id(0) == 0)
    def _():
        _load_weights_bf16([(wq_hbm, wq_ref), (wo_hbm, wo_ref), (wg_hbm, wg_ref), (wu_hbm, wu_ref),
                            (wd_hbm, wd_ref)], stage, sem)

    h = h_ref[...]
    d_model = h.shape[1]
    hd = d_model // X_HEADS
    qb = (_dot(h.astype(BF16), wq_ref[...]) * (hd ** -0.5)).astype(BF16)
    for hh in range(X_HEADS):
        cs = slice(hh * hd, (hh + 1) * hd)
        s = _dot_nt(qb[:, cs], k_ref[:, cs])
        p = jnp.exp(s - jnp.max(s, axis=-1, keepdims=True))
        l = jnp.sum(p, axis=-1, keepdims=True)
        a_scr[:, cs] = (_dot(p.astype(BF16), v_ref[:, cs]) / l).astype(BF16)
    groups = [slice(r, r + SUB_ROWS) for r in range(0, h.shape[0], SUB_ROWS)]
    h3 = [_layer_norm(alpha * h[rs, :] + _dot(a_scr[rs, :], wo_ref[...]), g3_ref[...], b3_ref[...])
          for rs in groups]
    for rs, h3_g in zip(groups, h3):
        o_ref[rs, :] = _swiglu_ln(h3_g, alpha, wg_ref, wu_ref, wd_ref, g4_ref, b4_ref, h_scr)


def _resident(shape):
    return pl.BlockSpec(shape, lambda *_: (0,) * len(shape), pipeline_mode=pl.Buffered(1))


_HBM = pl.BlockSpec(memory_space=pl.ANY)


def _weight_scratch(shapes):
    width = max(c for _, c in shapes)
    return ([pltpu.VMEM(shape, BF16) for shape in shapes]
            + [pltpu.VMEM((LOAD_SLOTS, LOAD_ROWS, width), F32), pltpu.SemaphoreType.DMA((LOAD_SLOTS,))])


def _params(n_grid_axes):
    return pltpu.CompilerParams(dimension_semantics=("arbitrary",) * n_grid_axes,
                                vmem_limit_bytes=VMEM_LIMIT_BYTES)


def kernel(x, mem, ffn1_w_gate, ffn1_w_up, ffn1_w_down, ln1_g, ln1_b, w_in, hg_lb_logits, hg_norm_g, sg_ln_g, sg_ln_b, sg_w_s, sg_b_s, w_out, ln2_g, ln2_b, mem_ln_g, mem_ln_b, xa_w_q, xa_w_k, xa_w_v, xa_w_o, ln3_g, ln3_b, ffn2_w_gate, ffn2_w_up, ffn2_w_down, ln4_g, ln4_b):
    depth = w_in.shape[0]
    assert depth == 1, "single-layer stack"
    B, T, D = x.shape
    n_rows = B * T
    d_ff = ffn1_w_gate.shape[2]
    alpha = (2.0 * depth) ** 0.25
    assert T % MIX_ROWS == 0 and T % FFN_ROWS == 0 and T % FFN1_ROWS == 0 and d_ff % FFN_COLS == 0
    assert MIX_ROWS % SG_CHUNK == 0 and MIX_ROWS % HG_CHUNK == 0
    assert D % LOAD_ROWS == 0 and d_ff % LOAD_ROWS == 0

    row = lambda a: a.reshape(1, -1)
    mat = lambda a: a.reshape(a.shape[1:])
    x2 = x.reshape(n_rows, D)
    mem2 = mem.reshape(B * MEM_LEN, D)

    k_mem, v_mem = pl.pallas_call(
        _kv_kernel,
        grid=(B,),
        in_specs=[pl.BlockSpec((MEM_LEN, D), lambda i: (i, 0)),
                  _resident((1, D)), _resident((1, D)), _HBM, _HBM],
        out_specs=[pl.BlockSpec((MEM_LEN, D), lambda i: (i, 0))] * 2,
        out_shape=[jax.ShapeDtypeStruct((B * MEM_LEN, D), BF16)] * 2,
        scratch_shapes=_weight_scratch([(D, D), (D, D)]),
        compiler_params=_params(1),
        name="memory_kv",
    )(mem2, row(mem_ln_g[0]), row(mem_ln_b[0]), mat(xa_w_k), mat(xa_w_v))

    row1_spec = pl.BlockSpec((FFN1_ROWS, D), lambda i: (i, 0))
    h1 = pl.pallas_call(
        functools.partial(_ffn_kernel, alpha),
        grid=(n_rows // FFN1_ROWS,),
        in_specs=[row1_spec, _HBM, _HBM, _HBM, _resident((1, D)), _resident((1, D))],
        out_specs=row1_spec,
        out_shape=jax.ShapeDtypeStruct((n_rows, D), F32),
        scratch_shapes=[pltpu.VMEM((FFN1_ROWS, d_ff), BF16)]
        + _weight_scratch([(D, d_ff), (D, d_ff), (d_ff, D)]),
        compiler_params=_params(1),
        name="ffn1_ln1",
    )(x2, mat(ffn1_w_gate), mat(ffn1_w_up), mat(ffn1_w_down), row(ln1_g[0]), row(ln1_b[0]))

    tiles_per_seq = T // MIX_ROWS
    in_width = w_in.shape[2]
    mix_spec = pl.BlockSpec((MIX_ROWS, D), lambda bi, ti: (bi * tiles_per_seq + ti, 0))
    h2 = pl.pallas_call(
        functools.partial(_mixer_kernel, alpha),
        grid=(B, tiles_per_seq),
        in_specs=[mix_spec, _HBM,
                  _resident((hg_lb_logits.shape[0], HG_WIDTH)), _resident((1, HG_DK)),
                  _resident((1, SG_WIDTH)), _resident((1, SG_WIDTH)),
                  _resident((SG_GROUPS, SG_CHUNK, SG_CHUNK)), _resident((SG_CHUNK, SG_GROUPS)),
                  _HBM, _resident((1, D)), _resident((1, D))],
        out_specs=mix_spec,
        out_shape=jax.ShapeDtypeStruct((n_rows, D), F32),
        scratch_shapes=[pltpu.VMEM((HG_HEADS, HG_DK, HG_DK), F32),
                        pltpu.VMEM((MIX_ROWS, HG_WIDTH), F32),
                        pltpu.VMEM((MIX_ROWS, D), BF16),
                        pltpu.VMEM((HG_HEADS * (MIX_ROWS // HG_CHUNK), HG_CHUNK, HG_CHUNK), BF16)]
        + _weight_scratch([(D, in_width), (D, D)]),
        compiler_params=_params(2),
        name="mixer_ln2",
    )(h1, mat(w_in), hg_lb_logits.reshape(hg_lb_logits.shape[0], HG_WIDTH), row(hg_norm_g[0]),
      row(sg_ln_g[0]), row(sg_ln_b[0]), sg_w_s[0], sg_b_s[0].T, mat(w_out),
      row(ln2_g[0]), row(ln2_b[0]))

    row_spec = pl.BlockSpec((FFN_ROWS, D), lambda i: (i, 0))
    tiles_per_batch = T // FFN_ROWS
    kv_spec = pl.BlockSpec((MEM_LEN, D), lambda i: (i // tiles_per_batch, 0))
    out = pl.pallas_call(
        functools.partial(_xattn_ffn_kernel, alpha),
        grid=(n_rows // FFN_ROWS,),
        in_specs=[row_spec, kv_spec, kv_spec, _HBM, _HBM, _resident((1, D)), _resident((1, D)),
                  _HBM, _HBM, _HBM, _resident((1, D)), _resident((1, D))],
        out_specs=row_spec,
        out_shape=jax.ShapeDtypeStruct((n_rows, D), F32),
        scratch_shapes=[pltpu.VMEM((FFN_ROWS, D), BF16), pltpu.VMEM((SUB_ROWS, d_ff), BF16)]
        + _weight_scratch([(D, D), (D, D), (D, d_ff), (D, d_ff), (d_ff, D)]),
        compiler_params=_params(1),
        name="xattn_ln3_ffn2_ln4",
    )(h2, k_mem, v_mem, mat(xa_w_q), mat(xa_w_o), row(ln3_g[0]), row(ln3_b[0]),
      mat(ffn2_w_gate), mat(ffn2_w_up), mat(ffn2_w_down), row(ln4_g[0]), row(ln4_b[0]))

    return out.reshape(B, T, D)
```

```python
import functools

import jax
import jax.numpy as jnp
from jax import lax
from jax.experimental import pallas as pl
from jax.experimental.pallas import tpu as pltpu

F32 = jnp.float32
BF16 = jnp.bfloat16

LN_EPS = 1e-5
GELU_SQRT_2_OVER_PI = 0.7978845608028654
GELU_CUBIC = 0.044715
MEM_LEN = 256
HG_HEADS = 4
HG_DK = 128
HG_WIDTH = HG_HEADS * HG_DK
SG_GROUPS = 4
SG_DIM = 128
SG_WIDTH = SG_GROUPS * SG_DIM
SG_CHUNK = 128
X_HEADS = 4

HG_CHUNK = 64
HG_LEVELS = (1, 2, 4, 8, 16, 32)
HG_SAFE_LOG2 = 100.0

FFN1_ROWS = 1024
FFN_ROWS = 512
SUB_ROWS = 256
FFN_COLS = 256
MIX_ROWS = 512
OUT_ROWS = 128
LOAD_ROWS = 128
LOAD_SLOTS = 4
VMEM_LIMIT_BYTES = 56 * 1024 * 1024
LANES = 128
BF16_TILE_ROWS = 16


def _dot(a, b):
    return jnp.dot(a, b, preferred_element_type=F32)


def _dot_nt(a, b):
    return lax.dot_general(a, b, (((1,), (1,)), ((), ())), preferred_element_type=F32)


def _dot_tn(a, b):
    return lax.dot_general(a, b, (((0,), (0,)), ((), ())), preferred_element_type=F32)


def _layer_norm(y, g, b):
    mu = jnp.mean(y, axis=-1, keepdims=True)
    d = y - mu
    var = jnp.mean(d * d, axis=-1, keepdims=True)
    return d * lax.rsqrt(var + LN_EPS) * g + b


def _sigmoid(x):
    return 1.0 / (1.0 + jnp.exp(-x))


def _load_weights_bf16(pairs, stage, sem):
    chunks = [(src, dst, r) for src, dst in pairs for r in range(0, src.shape[0], LOAD_ROWS)]
    ahead = LOAD_SLOTS - 1

    def chunk_copy(i):
        src, _, r = chunks[i]
        slot = i % LOAD_SLOTS
        return pltpu.make_async_copy(src.at[pl.ds(r, LOAD_ROWS), :],
                                     stage.at[slot, :, pl.ds(0, src.shape[1])], sem.at[slot])

    for i in range(min(ahead, len(chunks))):
        chunk_copy(i).start()
    for i, (src, dst, r) in enumerate(chunks):
        if i + ahead < len(chunks):
            chunk_copy(i + ahead).start()
        chunk_copy(i).wait()
        dst[pl.ds(r, LOAD_ROWS), :] = stage[i % LOAD_SLOTS, :, 0:src.shape[1]].astype(BF16)


def _after(x, deps):
    always = None
    for d in deps:
        c = (d == d) | (d != d)
        always = c if always is None else always & c
    t = jnp.where(always, -jnp.inf, deps[0]).astype(x.dtype)
    return jnp.maximum(x, jnp.tile(t, (x.shape[0] // BF16_TILE_ROWS, x.shape[1] // LANES)))


def _swiglu_ln(x, alpha, wg_ref, wu_ref, wd_ref, g_ref, b_ref, h_scr):
    xb = x.astype(BF16)
    d_ff = wg_ref.shape[1]
    for j in range(d_ff // FFN_COLS):
        sl = slice(j * FFN_COLS, (j + 1) * FFN_COLS)
        gate = _dot(xb, wg_ref[:, sl])
        up = _dot(xb, wu_ref[:, sl])
        h_scr[:, sl] = (gate * _sigmoid(gate) * up).astype(BF16)
    y = _dot(h_scr[...], wd_ref[...])
    return _layer_norm(alpha * x + 0.5 * y, g_ref[...], b_ref[...])


def _kv_kernel(mem_ref, g_ref, b_ref, wk_hbm, wv_hbm, k_ref, v_ref, wk_ref, wv_ref, stage, sem):
    @pl.when(pl.program_id(0) == 0)
    def _():
        _load_weights_bf16([(wk_hbm, wk_ref), (wv_hbm, wv_ref)], stage, sem)

    m = _layer_norm(mem_ref[...], g_ref[...], b_ref[...]).astype(BF16)
    k_ref[...] = _dot(m, wk_ref[...]).astype(BF16)
    v_ref[...] = _dot(m, wv_ref[...]).astype(BF16)


def _ffn_kernel(alpha, x_ref, wg_hbm, wu_hbm, wd_hbm, g_ref, b_ref, o_ref, h_scr,
                wg_ref, wu_ref, wd_ref, stage, sem):
    @pl.when(pl.program_id(0) == 0)
    def _():
        _load_weights_bf16([(wg_hbm, wg_ref), (wu_hbm, wu_ref), (wd_hbm, wd_ref)], stage, sem)

    for r in range(0, x_ref.shape[0], SUB_ROWS):
        rs = slice(r, r + SUB_ROWS)
        o_ref[rs, :] = _swiglu_ln(x_ref[rs, :], alpha, wg_ref, wu_ref, wd_ref, g_ref, b_ref, h_scr.at[rs, :])


def _bcast_row(a, blk, r):
    rows, w = a.shape
    a3 = a.reshape(rows // blk, blk, w)
    return jnp.broadcast_to(a3[:, r:r + 1, :], a3.shape).reshape(rows, w)


def _mixer_kernel(alpha, h_ref, w_in_hbm, lbl_ref, hgn_ref, sglg_ref, sglb_ref, ws_ref,
                  bst_ref, w_out_hbm, g_ref, b_ref, o_ref, st_ref, o_scr, mix_scr, sc_scr,
                  w_in_ref, w_out_ref, stage, sem):
    rows = h_ref.shape[0]
    W = HG_WIDTH

    @pl.when((pl.program_id(0) == 0) & (pl.program_id(1) == 0))
    def _():
        _load_weights_bf16([(w_in_hbm, w_in_ref), (w_out_hbm, w_out_ref)], stage, sem)

    first_of_seq = pl.program_id(1) == 0

    h = h_ref[...]
    hb = h.astype(BF16)
    fz = _dot(hb, w_in_ref[:, W:2 * W])
    uv = _dot(hb, w_in_ref[:, 4 * W:])

    lbl = lbl_ref[...]
    lbe = jnp.exp(lbl - jnp.max(lbl, axis=0, keepdims=True))
    lb = lbe[0:1, :] / jnp.sum(lbe, axis=0, keepdims=True)

    n_chunks = rows // HG_CHUNK
    blocks = [(hh, c) for hh in range(HG_HEADS) for c in range(n_chunks)]
    col = lambda hh: slice(hh * HG_DK, (hh + 1) * HG_DK)
    row_ = lambda c: slice(c * HG_CHUNK, (c + 1) * HG_CHUNK)

    rc = lax.broadcasted_iota(jnp.int32, (HG_CHUNK, W), 0)
    f_c, k_c, b_c = [], [], []
    for c in range(n_chunks):
        fzc = fz[row_(c), :]
        e = jnp.exp(-jnp.abs(fzc))
        r = 1.0 / (1.0 + e)
        er = e * r
        pos = fzc >= 0.0
        f = lb + (1.0 - lb) * jnp.where(pos, r, er)
        k_c.append((1.0 - lb) * jnp.where(pos, er, r))
        bc = jnp.log2(f)
        step = 1
        while step < HG_CHUNK:
            bc = bc + jnp.where(rc >= step, pltpu.roll(bc, step, axis=0), 0.0)
            step *= 2
        f_c.append(f)
        b_c.append(bc)

    q = _dot(_after(hb, [bc[HG_CHUNK - BF16_TILE_ROWS:, col(hh)] for bc in b_c for hh in range(HG_HEADS)]),
             w_in_ref[:, 0:W])
    ti = lax.broadcasted_iota(jnp.int32, (HG_CHUNK, HG_CHUNK), 0)
    si = lax.broadcasted_iota(jnp.int32, (HG_CHUNK, HG_CHUNK), 1)

    uvg = uv * (0.5 * (1.0 + jnp.tanh(GELU_SQRT_2_OVER_PI * (uv + GELU_CUBIC * (uv * uv * uv)))))
    causal = (lax.broadcasted_iota(jnp.int32, (SG_CHUNK, SG_CHUNK), 0)
              >= lax.broadcasted_iota(jnp.int32, (SG_CHUNK, SG_CHUNK), 1))
    sg_tail = []
    for g in range(SG_GROUPS):
        cs = slice(g * SG_DIM, (g + 1) * SG_DIM)
        u_g = uvg[:, cs]
        v_g = uvg[:, SG_WIDTH + g * SG_DIM:SG_WIDTH + (g + 1) * SG_DIM]
        vn = _layer_norm(v_g, sglg_ref[:, cs], sglb_ref[:, cs]).astype(BF16)
        w_g = jnp.where(causal, ws_ref[g], 0.0).astype(BF16)
        bias = bst_ref[:, g:g + 1]
        for n in range(rows // SG_CHUNK):
            rs = slice(n * SG_CHUNK, (n + 1) * SG_CHUNK)
            s_ = _dot(w_g, vn[rs, :]) + bias
            mix_scr[rs, HG_WIDTH + g * SG_DIM:HG_WIDTH + (g + 1) * SG_DIM] = (u_g[rs, :] * s_).astype(BF16)
        sg_tail.append(s_[0:BF16_TILE_ROWS, :])
    iv = _dot(_after(hb, sg_tail[:SG_GROUPS // 2]), w_in_ref[:, 2 * W:3 * W])
    gt = _dot(_after(hb, sg_tail[SG_GROUPS // 2:]), w_in_ref[:, 3 * W:4 * W])
    vb16 = iv.astype(BF16)

    gate_act = gt * _sigmoid(gt)

    q_st, dec_end, st_inc = [], [], {}
    worst = None
    for c in range(n_chunks):
        qc, kc, bc = q[row_(c), :], k_c[c], b_c[c]
        b_last = bc[HG_CHUNK - 1:HG_CHUNK, :]
        worst = -b_last if worst is None else jnp.maximum(worst, -b_last)
        q_st.append((qc * jnp.exp2(bc)).astype(BF16))
        k_st = (kc * jnp.exp2(b_last - bc)).astype(BF16)
        dec_end.append(jnp.exp2(b_last))
        x = bc - bc[HG_CHUNK // 2 - 1:HG_CHUNK // 2, :]
        qm = (qc * jnp.exp2(x)).astype(BF16)
        km = (kc * jnp.exp2(-x)).astype(BF16)
        for hh in range(HG_HEADS):
            st_inc[hh, c] = _dot_tn(vb16[row_(c), col(hh)], k_st[:, col(hh)])
            s_ = _dot_nt(qm[:, col(hh)], km[:, col(hh)])
            sc_scr[hh * n_chunks + c] = jnp.where(ti >= si, s_, 0.0).astype(BF16)
    safe = jnp.max(worst) < HG_SAFE_LOG2

    @pl.when(jnp.logical_not(safe))
    def _():
        k, f, b = (jnp.concatenate(parts, axis=0) for parts in (k_c, f_c, b_c))
        rw = lax.broadcasted_iota(jnp.int32, (rows, W), 0)
        lhs = [q.astype(BF16), jnp.where((rw % 2) == 1, q * f, k).astype(BF16)]
        for L in HG_LEVELS[1:]:
            if 2 * L >= 8:
                ref = _bcast_row(b, 2 * L, L - 1)
            else:
                ref = jnp.where((rw % 8) < 4, _bcast_row(b, 8, 1), _bcast_row(b, 8, 5))
            el = jnp.exp2(-jnp.abs(b - ref))
            upper = (rw % (2 * L)) >= L
            lhs.append((jnp.where(upper, q, k) * el).astype(BF16))
        rhs = [k.astype(BF16)] + lhs[1:]
        masks = [ti == si] + [((ti // (2 * L)) == (si // (2 * L))) & ((ti % (2 * L)) >= L) & ((si % (2 * L)) < L)
                              for L in HG_LEVELS]
        for i, (hh, c) in enumerate(blocks):
            s_ = jnp.zeros((HG_CHUNK, HG_CHUNK), F32)
            for a_, b_, m_ in zip(lhs, rhs, masks):
                s_ = jnp.where(m_, _dot_nt(a_[row_(c), col(hh)], b_[row_(c), col(hh)]), s_)
            sc_scr[i] = s_.astype(BF16)

    for hh in range(HG_HEADS):
        st = jnp.where(first_of_seq, 0.0, st_ref[hh])
        for c in range(n_chunks):
            o_scr[row_(c), col(hh)] = (_dot(sc_scr[hh * n_chunks + c], vb16[row_(c), col(hh)])
                                       + _dot_nt(q_st[c][:, col(hh)], st.astype(BF16)))
            st = st * dec_end[c][:, col(hh)] + st_inc[hh, c]
        st_ref[hh] = st

    o = o_scr[...]
    hgn = hgn_ref[...]
    for hh in range(HG_HEADS):
        cs = col(hh)
        oh = o[:, cs]
        ms = jnp.mean(oh * oh, axis=-1, keepdims=True)
        mix_scr[:, cs] = (oh * lax.rsqrt(ms + LN_EPS) * hgn * gate_act[:, cs]).astype(BF16)

    for r in range(0, rows, OUT_ROWS):
        rs = slice(r, r + OUT_ROWS)
        mix = _dot(mix_scr[rs, :], w_out_ref[...])
        o_ref[rs, :] = _layer_norm(alpha * h[rs, :] + mix, g_ref[...], b_ref[...])


def _xattn_ffn_kernel(alpha, h_ref, k_ref, v_ref, wq_hbm, wo_hbm, g3_ref, b3_ref,
                      wg_hbm, wu_hbm, wd_hbm, g4_ref, b4_ref, o_ref, a_scr, h_scr,
                      wq_ref, wo_ref, wg_ref, wu_ref, wd_ref, stage, sem):
    @pl.when(pl.program_id(0) == 0)
    def _():
        _load_weights_bf16([(wq_hbm, wq_ref), (wo_hbm, wo_ref), (wg_hbm, wg_ref), (wu_hbm, wu_ref),
                            (wd_hbm, wd_ref)], stage, sem)

    h = h_ref[...]
    d_model = h.shape[1]
    hd = d_model // X_HEADS
    qb = (_dot(h.astype(BF16), wq_ref[...]) * (hd ** -0.5)).astype(BF16)
    for hh in range(X_HEADS):
        cs = slice(hh * hd, (hh + 1) * hd)
        s = _dot_nt(qb[:, cs], k_ref[:, cs])
        p = jnp.exp(s - jnp.max(s, axis=-1, keepdims=True))
        l = jnp.sum(p, axis=-1, keepdims=True)
        a_scr[:, cs] = (_dot(p.astype(BF16), v_ref[:, cs]) / l).astype(BF16)
    groups = [slice(r, r + SUB_ROWS) for r in range(0, h.shape[0], SUB_ROWS)]
    h3 = [_layer_norm(alpha * h[rs, :] + _dot(a_scr[rs, :], wo_ref[...]), g3_ref[...], b3_ref[...])
          for rs in groups]
    for rs, h3_g in zip(groups, h3):
        o_ref[rs, :] = _swiglu_ln(h3_g, alpha, wg_ref, wu_ref, wd_ref, g4_ref, b4_ref, h_scr.at[rs, :])


def _resident(shape):
    return pl.BlockSpec(shape, lambda *_: (0,) * len(shape), pipeline_mode=pl.Buffered(1))


_HBM = pl.BlockSpec(memory_space=pl.ANY)


def _weight_scratch(shapes):
    width = max(c for _, c in shapes)
    return ([pltpu.VMEM(shape, BF16) for shape in shapes]
            + [pltpu.VMEM((LOAD_SLOTS, LOAD_ROWS, width), F32), pltpu.SemaphoreType.DMA((LOAD_SLOTS,))])


def _params(n_grid_axes):
    return pltpu.CompilerParams(dimension_semantics=("arbitrary",) * n_grid_axes,
                                vmem_limit_bytes=VMEM_LIMIT_BYTES)


def kernel(x, mem, ffn1_w_gate, ffn1_w_up, ffn1_w_down, ln1_g, ln1_b, w_in, hg_lb_logits, hg_norm_g, sg_ln_g, sg_ln_b, sg_w_s, sg_b_s, w_out, ln2_g, ln2_b, mem_ln_g, mem_ln_b, xa_w_q, xa_w_k, xa_w_v, xa_w_o, ln3_g, ln3_b, ffn2_w_gate, ffn2_w_up, ffn2_w_down, ln4_g, ln4_b):
    depth = w_in.shape[0]
    assert depth == 1, "single-layer stack"
    B, T, D = x.shape
    n_rows = B * T
    d_ff = ffn1_w_gate.shape[2]
    alpha = (2.0 * depth) ** 0.25
    assert T % MIX_ROWS == 0 and T % FFN_ROWS == 0 and T % FFN1_ROWS == 0 and d_ff % FFN_COLS == 0
    assert MIX_ROWS % SG_CHUNK == 0 and MIX_ROWS % HG_CHUNK == 0
    assert D % LOAD_ROWS == 0 and d_ff % LOAD_ROWS == 0

    row = lambda a: a.reshape(1, -1)
    mat = lambda a: a.reshape(a.shape[1:])
    x2 = x.reshape(n_rows, D)
    mem2 = mem.reshape(B * MEM_LEN, D)

    k_mem, v_mem = pl.pallas_call(
        _kv_kernel,
        grid=(B,),
        in_specs=[pl.BlockSpec((MEM_LEN, D), lambda i: (i, 0)),
                  _resident((1, D)), _resident((1, D)), _HBM, _HBM],
        out_specs=[pl.BlockSpec((MEM_LEN, D), lambda i: (i, 0))] * 2,
        out_shape=[jax.ShapeDtypeStruct((B * MEM_LEN, D), BF16)] * 2,
        scratch_shapes=_weight_scratch([(D, D), (D, D)]),
        compiler_params=_params(1),
        name="memory_kv",
    )(mem2, row(mem_ln_g[0]), row(mem_ln_b[0]), mat(xa_w_k), mat(xa_w_v))

    row1_spec = pl.BlockSpec((FFN1_ROWS, D), lambda i: (i, 0))
    h1 = pl.pallas_call(
        functools.partial(_ffn_kernel, alpha),
        grid=(n_rows // FFN1_ROWS,),
        in_specs=[row1_spec, _HBM, _HBM, _HBM, _resident((1, D)), _resident((1, D))],
        out_specs=row1_spec,
        out_shape=jax.ShapeDtypeStruct((n_rows, D), F32),
        scratch_shapes=[pltpu.VMEM((FFN1_ROWS, d_ff), BF16)]
        + _weight_scratch([(D, d_ff), (D, d_ff), (d_ff, D)]),
        compiler_params=_params(1),
        name="ffn1_ln1",
    )(x2, mat(ffn1_w_gate), mat(ffn1_w_up), mat(ffn1_w_down), row(ln1_g[0]), row(ln1_b[0]))

    tiles_per_seq = T // MIX_ROWS
    in_width = w_in.shape[2]
    mix_spec = pl.BlockSpec((MIX_ROWS, D), lambda bi, ti: (bi * tiles_per_seq + ti, 0))
    h2 = pl.pallas_call(
        functools.partial(_mixer_kernel, alpha),
        grid=(B, tiles_per_seq),
        in_specs=[mix_spec, _HBM,
                  _resident((hg_lb_logits.shape[0], HG_WIDTH)), _resident((1, HG_DK)),
                  _resident((1, SG_WIDTH)), _resident((1, SG_WIDTH)),
                  _resident((SG_GROUPS, SG_CHUNK, SG_CHUNK)), _resident((SG_CHUNK, SG_GROUPS)),
                  _HBM, _resident((1, D)), _resident((1, D))],
        out_specs=mix_spec,
        out_shape=jax.ShapeDtypeStruct((n_rows, D), F32),
        scratch_shapes=[pltpu.VMEM((HG_HEADS, HG_DK, HG_DK), F32),
                        pltpu.VMEM((MIX_ROWS, HG_WIDTH), F32),
                        pltpu.VMEM((MIX_ROWS, D), BF16),
                        pltpu.VMEM((HG_HEADS * (MIX_ROWS // HG_CHUNK), HG_CHUNK, HG_CHUNK), BF16)]
        + _weight_scratch([(D, in_width), (D, D)]),
        compiler_params=_params(2),
        name="mixer_ln2",
    )(h1, mat(w_in), hg_lb_logits.reshape(hg_lb_logits.shape[0], HG_WIDTH), row(hg_norm_g[0]),
      row(sg_ln_g[0]), row(sg_ln_b[0]), sg_w_s[0], sg_b_s[0].T, mat(w_out),
      row(ln2_g[0]), row(ln2_b[0]))

    row_spec = pl.BlockSpec((FFN_ROWS, D), lambda i: (i, 0))
    tiles_per_batch = T // FFN_ROWS
    kv_spec = pl.BlockSpec((MEM_LEN, D), lambda i: (i // tiles_per_batch, 0))
    out = pl.pallas_call(
        functools.partial(_xattn_ffn_kernel, alpha),
        grid=(n_rows // FFN_ROWS,),
        in_specs=[row_spec, kv_spec, kv_spec, _HBM, _HBM, _resident((1, D)), _resident((1, D)),
                  _HBM, _HBM, _HBM, _resident((1, D)), _resident((1, D))],
        out_specs=row_spec,
        out_shape=jax.ShapeDtypeStruct((n_rows, D), F32),
        scratch_shapes=[pltpu.VMEM((FFN_ROWS, D), BF16), pltpu.VMEM((FFN_ROWS, d_ff), BF16)]
        + _weight_scratch([(D, D), (D, D), (D, d_ff), (D, d_ff), (d_ff, D)]),
        compiler_params=_params(1),
        name="xattn_ln3_ffn2_ln4",
    )(h2, k_mem, v_mem, mat(xa_w_q), mat(xa_w_o), row(ln3_g[0]), row(ln3_b[0]),
      mat(ffn2_w_gate), mat(ffn2_w_up), mat(ffn2_w_down), row(ln4_g[0]), row(ln4_b[0]))

    return out.reshape(B, T, D)
```
